```python
import jax, jax.numpy as jnp
from jax import lax
import numpy as np

D_MODEL = 1024
BATCH = 8
SEQ = 2048
DEPTH = 4
DEC_BATCH = 128
DEC_SEQ = 8
PAST_LEN = 16384
PAGE_SIZE = 128

A_HEADS = 8
A_DK = 128
A_DV = D_MODEL // A_HEADS
A_WIDTH_K = A_HEADS * A_DK
A_WIDTH_V = A_HEADS * A_DV
B_HEADS = 4
B_WIDTH = D_MODEL
B_DH = B_WIDTH // B_HEADS
QK_BLOCK = 4
N_QK_BLOCKS = B_WIDTH // QK_BLOCK
CONV_W = 4
CHUNK = 32
D_FF = 2816
N_EXPERTS = 8
TOP_K = 2
N_DENSE = (DEPTH + 1) // 2
N_MOE = DEPTH // 2
ALPHA = (2 * DEPTH) ** 0.25
BETA = (8 * DEPTH) ** -0.25
EPS = 1e-5
NEG = -1e30
TINY = 1e-30
IN_SIZES = (A_WIDTH_K, A_WIDTH_K, A_WIDTH_V, A_WIDTH_V, B_WIDTH, B_WIDTH, B_WIDTH, B_HEADS, B_HEADS, D_MODEL, D_MODEL)
N_IN = sum(IN_SIZES)

kernel_name = 'hgrn2_mlstm_gated_parallel_deepnorm_moe_step'


def _split_points():
    pts, acc = [], 0
    for s in IN_SIZES[:-1]:
        acc += s
        pts.append(acc)
    return tuple(pts)


def layer_norm(x, g, b):
    xf = x.astype(jnp.float32)
    mu = jnp.mean(xf, -1, keepdims=True)
    xc = xf - mu
    var = jnp.mean(xc * xc, -1, keepdims=True)
    return (xc * lax.rsqrt(var + EPS) * g + b).astype(x.dtype)


def rms_norm_heads(x, g):
    return x * lax.rsqrt(jnp.mean(x * x, -1, keepdims=True) + EPS) * g


def layer_norm_heads(x, g):
    xc = x - jnp.mean(x, -1, keepdims=True)
    return xc * lax.rsqrt(jnp.mean(xc * xc, -1, keepdims=True) + EPS) * g


def chunk_len(l):
    return CHUNK if l % CHUNK == 0 else l


def to_chunks(t, c):
    b, l = t.shape[:2]
    t = t.reshape((b, l // c, c) + t.shape[2:])
    return jnp.moveaxis(t, (1, 2), (0, 3))


def from_chunks(t):
    n, b, h, c, d = t.shape
    return jnp.moveaxis(t, (0, 3), (1, 2)).reshape(b, n * c, h, d)


def hgrn2_scan(q, k, log_f, v, s0):
    c = chunk_len(q.shape[1])
    causal = jnp.tril(jnp.ones((c, c), bool))

    def step(s, inp):
        qi, ki, gi, vi = inp
        G = jnp.cumsum(gi, axis=2)
        expo = G[:, :, :, None, :] - G[:, :, None, :, :]
        w = jnp.exp(jnp.where(causal[:, :, None], expo, NEG))
        att = jnp.einsum('bhtd,bhsd,bhtsd->bhts', qi, ki, w)
        o = jnp.einsum('bhts,bhsv->bhtv', att, vi) + jnp.einsum('bhtd,bhdv->bhtv', qi * jnp.exp(G), s)
        g_last = G[:, :, -1:, :]
        s_new = jnp.exp(g_last[:, :, 0, :])[..., None] * s + jnp.einsum('bhsd,bhsv->bhdv', ki * jnp.exp(g_last - G), vi)
        return s_new, o

    s_fin, oc = lax.scan(step, s0, (to_chunks(q, c), to_chunks(k, c), to_chunks(log_f, c), to_chunks(v, c)))
    return from_chunks(oc), s_fin


def mlstm_scan(q, k, v, i_pre, log_fg, c0, n0, m0):
    c = chunk_len(q.shape[1])
    causal = jnp.tril(jnp.ones((c, c), bool))

    def step(carry, inp):
        C, n, m = carry
        qi, ki, vi, ii, fi = inp
        b = jnp.cumsum(fi, axis=-1)
        d_intra = jnp.where(causal, b[..., :, None] - b[..., None, :] + ii[..., None, :], NEG)
        d_inter = b + m[..., None]
        m_t = jnp.maximum(jnp.max(d_intra, -1), d_inter)
        w = jnp.exp(d_intra - m_t[..., None])
        w_inter = jnp.exp(d_inter - m_t)
        s = jnp.einsum('bhtd,bhsd->bhts', qi, ki) * w
        num = jnp.einsum('bhts,bhsv->bhtv', s, vi) + w_inter[..., None] * jnp.einsum('bhtd,bhdv->bhtv', qi, C)
        den = jnp.sum(s, -1) + w_inter * jnp.einsum('bhtd,bhd->bht', qi, n)
        h = num / jnp.maximum(jnp.abs(den), jnp.exp(-m_t))[..., None]
        m_new = m_t[..., -1]
        w_state = jnp.exp(b[..., -1:] - b + ii - m_new[..., None])
        decay = jnp.exp(b[..., -1] + m - m_new)
        C_new = decay[..., None, None] * C + jnp.einsum('bhs,bhsd,bhsv->bhdv', w_state, ki, vi)
        n_new = decay[..., None] * n + jnp.einsum('bhs,bhsd->bhd', w_state, ki)
        return (C_new, n_new, m_new), h

    (c_f, n_f, m_f), hc = lax.scan(step, (c0, n0, m0), (to_chunks(q, c), to_chunks(k, c), to_chunks(v, c), to_chunks(i_pre, c), to_chunks(log_fg, c)))
    return from_chunks(hc), c_f, n_f, m_f


def causal_conv(u, buf, w, bias):
    xp = jnp.concatenate([buf.astype(u.dtype), u], axis=1)
    l = u.shape[1]
    y = sum(xp[:, j:j + l] * w[j] for j in range(CONV_W)) + bias
    return y, xp[:, -(CONV_W - 1):]


def token_mixer(x, s_hgrn, c_m, n_m, m_m, conv_buf, w_in, lb, a_norm_g, w_pa, conv_w, conv_b,
                w_q_blk, w_k_blk, b_ig, b_fg, b_norm_g, w_pb, w_o):
    f32 = jnp.float32
    bsz, seq, _ = x.shape
    z = x @ w_in
    aq, af, ai, ag, bu, bv, bo, big, bfg, ga, gb = jnp.split(z, _split_points(), axis=-1)
    lbh = lb.reshape(A_HEADS, A_DK)
    fp = af.astype(f32).reshape(bsz, seq, A_HEADS, A_DK)
    f_gate = lbh + (1.0 - lbh) * jax.nn.sigmoid(fp)
    log_f = jnp.log(jnp.maximum(f_gate, TINY))
    key_a = (1.0 - lbh) * jax.nn.sigmoid(-fp)
    q_a = jax.nn.silu(aq.astype(f32)).reshape(bsz, seq, A_HEADS, A_DK)
    v_a = ai.astype(f32).reshape(bsz, seq, A_HEADS, A_DV)
    o_a, s_new = hgrn2_scan(q_a, key_a, log_f, v_a, s_hgrn.astype(f32))
    o_a = rms_norm_heads(o_a, a_norm_g.reshape(A_HEADS, A_DV)) * jax.nn.silu(ag.astype(f32)).reshape(bsz, seq, A_HEADS, A_DV)
    y_a = o_a.reshape(bsz, seq, A_WIDTH_V).astype(x.dtype) @ w_pa
    u, conv_new = causal_conv(bu, conv_buf, conv_w, conv_b)
    u = jax.nn.silu(u).reshape(bsz, seq, N_QK_BLOCKS, QK_BLOCK)
    q_b = jnp.einsum('blni,nij->blnj', u, w_q_blk).reshape(bsz, seq, B_HEADS, B_DH).astype(f32)
    k_b = jnp.einsum('blni,nij->blnj', u, w_k_blk).reshape(bsz, seq, B_HEADS, B_DH).astype(f32) * (B_DH ** -0.5)
    v_b = bv.astype(f32).reshape(bsz, seq, B_HEADS, B_DH)
    i_pre = big.astype(f32) + b_ig.astype(f32)
    log_fg = jax.nn.log_sigmoid(bfg.astype(f32) + b_fg.astype(f32))
    h_b, c_new, n_new, m_new = mlstm_scan(q_b, k_b, v_b, i_pre, log_fg, c_m.astype(f32), n_m.astype(f32), m_m.astype(f32))
    h_b = jax.nn.sigmoid(bo.astype(f32)).reshape(bsz, seq, B_HEADS, B_DH) * layer_norm_heads(h_b, b_norm_g.reshape(B_HEADS, B_DH))
    y_b = h_b.reshape(bsz, seq, B_WIDTH).astype(x.dtype) @ w_pb
    merged = jax.nn.sigmoid(ga) * y_a + jax.nn.sigmoid(gb) * y_b
    out = merged @ w_o
    dt = x.dtype
    return out, (s_new.astype(dt), c_new.astype(dt), n_new.astype(dt), m_new.astype(dt), conv_new.astype(dt))


def swiglu(x, w_gate, w_up, w_down):
    return (jax.nn.silu(x @ w_gate) * (x @ w_up)) @ w_down


def moe_swiglu(x, w_router, b_router, w_gate, w_up, w_down):
    logits = (x @ w_router).astype(jnp.float32) + b_router.astype(jnp.float32)
    top_val, top_idx = lax.top_k(logits, TOP_K)
    top_w = jax.nn.softmax(top_val, axis=-1)
    gate = jnp.einsum('blk,blke->ble', top_w, jax.nn.one_hot(top_idx, N_EXPERTS, dtype=jnp.float32)).astype(x.dtype)
    out = jnp.zeros_like(x)
    for e in range(N_EXPERTS):
        out = out + gate[..., e:e + 1] * swiglu(x, w_gate[e], w_up[e], w_down[e])
    return out


def setup_inputs(seed: int = 0) -> dict:
    key = jax.random.key(seed)
    k = jax.random.split(key, 40)
    f32 = jnp.float32

    def nrm(i, shape, scale):
        return jax.random.normal(k[i], shape, f32) * scale

    part_scale = [D_MODEL ** -0.5] * len(IN_SIZES)
    part_scale[2] *= BETA
    part_scale[5] *= BETA
    col_scale = jnp.asarray(np.repeat(np.array(part_scale, np.float32), np.array(IN_SIZES)))
    return {
        'x_prompt': nrm(0, (BATCH, SEQ, D_MODEL), 1.0),
        'x_sample': nrm(1, (DEC_BATCH, DEC_SEQ, D_MODEL), 1.0),
        'state_hgrn': nrm(2, (DEPTH, DEC_BATCH, A_HEADS, A_DK, A_DV), 0.3),
        'state_mlstm_c': nrm(3, (DEPTH, DEC_BATCH, B_HEADS, B_DH, B_DH), 0.05),
        'state_mlstm_n': nrm(4, (DEPTH, DEC_BATCH, B_HEADS, B_DH), 0.1),
        'state_mlstm_m': jax.random.uniform(k[5], (DEPTH, DEC_BATCH, B_HEADS), f32, 0.0, 4.0),
        'state_conv': nrm(6, (DEPTH, DEC_BATCH, CONV_W - 1, B_WIDTH), 1.0),
        'w_in': nrm(7, (DEPTH, D_MODEL, N_IN), 1.0) * col_scale,
        'lb_param': nrm(8, (DEPTH, A_WIDTH_K), 0.5),
        'a_norm_g': 1.0 + nrm(9, (DEPTH, A_WIDTH_V), 0.02),
        'w_pa': nrm(10, (DEPTH, A_WIDTH_V, D_MODEL), A_WIDTH_V ** -0.5),
        'conv_w': nrm(11, (DEPTH, CONV_W, B_WIDTH), CONV_W ** -0.5),
        'conv_b': nrm(12, (DEPTH, B_WIDTH), 0.02),
        'w_q_blk': nrm(13, (DEPTH, N_QK_BLOCKS, QK_BLOCK, QK_BLOCK), QK_BLOCK ** -0.5),
        'w_k_blk': nrm(14, (DEPTH, N_QK_BLOCKS, QK_BLOCK, QK_BLOCK), QK_BLOCK ** -0.5),
        'b_ig': nrm(15, (DEPTH, B_HEADS), 0.1),
        'b_fg': jnp.linspace(3.0, 6.0, B_HEADS, dtype=f32)[None, :] + nrm(16, (DEPTH, B_HEADS), 0.1),
        'b_norm_g': 1.0 + nrm(17, (DEPTH, B_WIDTH), 0.02),
        'w_pb': nrm(18, (DEPTH, B_WIDTH, D_MODEL), B_WIDTH ** -0.5),
        'w_o': nrm(19, (DEPTH, D_MODEL, D_MODEL), BETA * D_MODEL ** -0.5),
        'ln1_g': 1.0 + nrm(20, (DEPTH, D_MODEL), 0.02),
        'ln1_b': nrm(21, (DEPTH, D_MODEL), 0.02),
        'ln2_g': 1.0 + nrm(22, (DEPTH, D_MODEL), 0.02),
        'ln2_b': nrm(23, (DEPTH, D_MODEL), 0.02),
        'w_ff_gate': nrm(24, (N_DENSE, D_MODEL, D_FF), D_MODEL ** -0.5),
        'w_ff_up': nrm(25, (N_DENSE, D_MODEL, D_FF), D_MODEL ** -0.5),
        'w_ff_down': nrm(26, (N_DENSE, D_FF, D_MODEL), BETA * D_FF ** -0.5),
        'w_router': nrm(27, (N_MOE, D_MODEL, N_EXPERTS), D_MODEL ** -0.5),
        'b_router': nrm(28, (N_MOE, N_EXPERTS), 0.01),
        'w_ex_gate': nrm(29, (N_MOE, N_EXPERTS, D_MODEL, D_FF), D_MODEL ** -0.5),
        'w_ex_up': nrm(30, (N_MOE, N_EXPERTS, D_MODEL, D_FF), D_MODEL ** -0.5),
        'w_ex_down': nrm(31, (N_MOE, N_EXPERTS, D_FF, D_MODEL), BETA * D_FF ** -0.5),
    }


def reference(x_prompt, x_sample, state_hgrn, state_mlstm_c, state_mlstm_n, state_mlstm_m, state_conv,
              w_in, lb_param, a_norm_g, w_pa, conv_w, conv_b, w_q_blk, w_k_blk, b_ig, b_fg, b_norm_g, w_pb, w_o,
              ln1_g, ln1_b, ln2_g, ln2_b, w_ff_gate, w_ff_up, w_ff_down,
              w_router, b_router, w_ex_gate, w_ex_up, w_ex_down):
    lb = jax.nn.softmax(lb_param.astype(jnp.float32), axis=0)
    lb = jnp.cumsum(lb, axis=0) - lb[:1]
    bp = x_prompt.shape[0]
    dt = x_prompt.dtype
    y_p, y_s = x_prompt, x_sample
    new_p = ([], [], [], [], [])
    new_s = ([], [], [], [], [])
    for l in range(DEPTH):
        mix_w = (w_in[l], lb[l], a_norm_g[l], w_pa[l], conv_w[l], conv_b[l], w_q_blk[l], w_k_blk[l],
                 b_ig[l], b_fg[l], b_norm_g[l], w_pb[l], w_o[l])
        zero_state = (jnp.zeros((bp, A_HEADS, A_DK, A_DV), dt), jnp.zeros((bp, B_HEADS, B_DH, B_DH), dt),
                      jnp.zeros((bp, B_HEADS, B_DH), dt), jnp.zeros((bp, B_HEADS), dt),
                      jnp.zeros((bp, CONV_W - 1, B_WIDTH), dt))
        past_state = (state_hgrn[l], state_mlstm_c[l], state_mlstm_n[l], state_mlstm_m[l], state_conv[l])
        outs = []
        for x, st, acc in ((y_p, zero_state, new_p), (y_s, past_state, new_s)):
            mixed, st_new = token_mixer(x, *st, *mix_w)
            x = layer_norm(ALPHA * x + mixed, ln1_g[l], ln1_b[l])
            j = l // 2
            if l % 2 == 0:
                ff = swiglu(x, w_ff_gate[j], w_ff_up[j], w_ff_down[j])
            else:
                ff = moe_swiglu(x, w_router[j], b_router[j], w_ex_gate[j], w_ex_up[j], w_ex_down[j])
            x = layer_norm(ALPHA * x + ff, ln2_g[l], ln2_b[l])
            for a, s in zip(acc, st_new):
                a.append(s)
            outs.append(x)
        y_p, y_s = outs
    hgrn_p, c_p, n_p, m_p, conv_p = (jnp.stack(a) for a in new_p)
    hgrn_s, c_s, n_s, m_s, conv_s = (jnp.stack(a) for a in new_s)
    return (y_p, y_s, hgrn_p, hgrn_s, c_p, c_s, n_p, n_s, m_p, m_s, conv_p, conv_s)
```

```python
import functools

import jax
import jax.numpy as jnp
from jax import lax
from jax.experimental import pallas as pl
from jax.experimental.pallas import tpu as pltpu

F32 = jnp.float32
BF16 = jnp.bfloat16

D_MODEL = 1024
A_HEADS = 8
A_DK = 128
A_DV = 128
B_HEADS = 4
B_DH = 256
QK_BLOCK = 4
CONV_W = 4
CHUNK = 32
D_FF = 2816
N_EXPERTS = 8
DEPTH = 4
ALPHA = (2 * DEPTH) ** 0.25
EPS = 1e-5
NEG = -1e30
TINY = 1e-30

LANES = 128
SUBLANES = 8
VMEM_LIMIT = 56 * 1024 * 1024

COL_AQ, COL_AF, COL_AI, COL_AG, COL_BU, COL_BV, COL_BO, COL_GA, COL_GB = range(9)
N_MAIN_BLOCKS = 9


def _pick_tile(n, prefs):
    for p in prefs:
        if n % p == 0:
            return p
    return n


def _sigmoid(x):
    return 1.0 / (1.0 + jnp.exp(-x))


def _split3(x):
    hi = x.astype(BF16)
    r1 = x - hi.astype(F32)
    mid = r1.astype(BF16)
    lo = (r1 - mid.astype(F32)).astype(BF16)
    return hi, mid, lo


def _cumsum_rows(tril_bf16, x):
    hi, mid, lo = _split3(x)
    acc = jnp.dot(tril_bf16, lo, preferred_element_type=F32)
    acc = acc + jnp.dot(tril_bf16, mid, preferred_element_type=F32)
    return acc + jnp.dot(tril_bf16, hi, preferred_element_type=F32)


def _dot_nt(a, b):
    return lax.dot_general(a, b, (((1,), (1,)), ((), ())), preferred_element_type=F32)


def _dot_tn(a, b):
    return lax.dot_general(a, b, (((0,), (0,)), ((), ())), preferred_element_type=F32)


def _proj_in_kernel(x_ref, w_ref, wg_ref, z_ref, g_ref, xb_ref):
    @pl.when(pl.program_id(1) == 0)
    def _():
        xb = x_ref[...].astype(BF16)
        xb_ref[...] = xb
        g_ref[...] = jnp.dot(xb, wg_ref[...], preferred_element_type=F32)

    z_ref[...] = jnp.dot(xb_ref[...], w_ref[...], preferred_element_type=F32)


def _proj_in(x, w_main, w_gate):
    t = x.shape[0]
    tm = _pick_tile(t, (1024, 512, 256, 128, 64, 32, 16, 8))
    return pl.pallas_call(
        _proj_in_kernel,
        grid=(t // tm, N_MAIN_BLOCKS),
        in_specs=[
            pl.BlockSpec((tm, D_MODEL), lambda i, j: (i, 0)),
            pl.BlockSpec((D_MODEL, D_MODEL), lambda i, j: (0, j)),
            pl.BlockSpec((D_MODEL, LANES), lambda i, j: (0, 0)),
        ],
        out_specs=[
            pl.BlockSpec((tm, D_MODEL), lambda i, j: (i, j)),
            pl.BlockSpec((tm, LANES), lambda i, j: (i, 0)),
        ],
        out_shape=[
            jax.ShapeDtypeStruct((t, N_MAIN_BLOCKS * D_MODEL), F32),
            jax.ShapeDtypeStruct((t, LANES), F32),
        ],
        scratch_shapes=[pltpu.VMEM((tm, D_MODEL), BF16)],
        compiler_params=pltpu.CompilerParams(
            dimension_semantics=("arbitrary", "arbitrary"), vmem_limit_bytes=VMEM_LIMIT),
        name="proj_in",
    )(x, w_main, w_gate)


def _mixer_kernel(*refs, layer, lt, c, has_state, nsteps):
    (aq_ref, af_ref, ai_ref, ag_ref, bu_ref, bv_ref, bo_ref, gt_ref,
     lbp_ref, ang_ref, cw_ref, cb_ref, wq_ref, wk_ref, gbias_ref, bng_ref) = refs[:16]
    pos = 16
    if has_state:
        s0_ref, c0_ref, n0_ref, m0_ref, cv0_ref = refs[pos:pos + 5]
        pos += 5
    oa_ref, hb_ref, so_ref, co_ref, no_ref, mo_ref, cvo_ref = refs[pos:pos + 7]
    pos += 7
    st_ref, c_ref, n_ref, m_ref, xp_ref, qb_ref, kb_ref = refs[pos:]

    step = pl.program_id(1)
    halo = SUBLANES

    @pl.when(step == 0)
    def _():
        if has_state:
            for h in range(A_HEADS):
                st_ref[h] = s0_ref[0, h].T
            c_ref[...] = c0_ref[0]
            n_ref[0:B_HEADS, :] = n0_ref[0]
            for h in range(B_HEADS):
                m_ref[h] = jnp.broadcast_to(m0_ref[0, h:h + 1, :], (SUBLANES, LANES))
            xp_ref[0:halo, :] = jnp.zeros((halo, D_MODEL), F32)
            xp_ref[halo - (CONV_W - 1):halo, :] = cv0_ref[0]
        else:
            st_ref[...] = jnp.zeros_like(st_ref)
            c_ref[...] = jnp.zeros_like(c_ref)
            n_ref[...] = jnp.zeros_like(n_ref)
            m_ref[...] = jnp.zeros_like(m_ref)
            xp_ref[0:halo, :] = jnp.zeros((halo, D_MODEL), F32)

    xp_ref[halo:halo + lt, :] = bu_ref[...]
    y = cb_ref[...]
    for j in range(CONV_W):
        y = y + xp_ref[halo - (CONV_W - 1) + j:halo - (CONV_W - 1) + j + lt, :] * cw_ref[j:j + 1, :]
    tail = xp_ref[halo + lt - (CONV_W - 1):halo + lt, :]
    xp_ref[halo - (CONV_W - 1):halo, :] = tail
    u = (y * _sigmoid(y)).astype(BF16)
    for g in range(B_HEADS):
        sl = slice(g * B_DH, (g + 1) * B_DH)
        qb_ref[:, sl] = jnp.dot(u[:, sl], wq_ref[g], preferred_element_type=F32)
        kb_ref[:, sl] = jnp.dot(u[:, sl], wk_ref[g], preferred_element_type=F32) * (B_DH ** -0.5)

    lbp = lbp_ref[...]
    lbe = jnp.exp(lbp - jnp.max(lbp, axis=0, keepdims=True))
    lbs = lbe / jnp.sum(lbe, axis=0, keepdims=True)
    lb = jnp.zeros((1, D_MODEL), F32)
    for r in range(1, layer + 1):
        lb = lb + lbs[r:r + 1, :]
    one_m_lb = 1.0 - lb

    row = lax.broadcasted_iota(jnp.int32, (c, c), 0)
    col = lax.broadcasted_iota(jnp.int32, (c, c), 1)
    causal = row >= col
    tril = jnp.where(causal, 1.0, 0.0).astype(BF16)
    lane = lax.broadcasted_iota(jnp.int32, (c, LANES), 1)
    rows8 = lax.broadcasted_iota(jnp.int32, (SUBLANES, LANES), 0)
    nrb = c // SUBLANES

    def chunk(r0):
        rs = pl.ds(r0, c)
        fp = af_ref[rs, :]
        e = jnp.exp(-jnp.abs(fp))
        r = 1.0 / (1.0 + e)
        er = e * r
        sig_p = jnp.where(fp >= 0, r, er)
        sig_n = jnp.where(fp >= 0, er, r)
        f_gate = lb + one_m_lb * sig_p
        log_f = jnp.log(jnp.maximum(f_gate, TINY))
        key = one_m_lb * sig_n
        aq = aq_ref[rs, :]
        q = aq * _sigmoid(aq)
        v = ai_ref[rs, :]
        gcum = _cumsum_rows(tril, log_f)
        g_last = gcum[c - 1:c, :]
        qe = (q * jnp.exp(gcum)).astype(BF16)
        kd = (key * jnp.exp(g_last - gcum)).astype(BF16)
        dec = jnp.exp(g_last)
        vb = v.astype(BF16)
        ag = ag_ref[rs, :]
        out_gate = ag * _sigmoid(ag)
        ang = ang_ref[...]
        for h in range(A_HEADS):
            hs = slice(h * A_DK, (h + 1) * A_DK)
            gh, qh, kh, vh = gcum[:, hs], q[:, hs], key[:, hs], v[:, hs]
            acc = [None] * nrb
            for s in range(c):
                rb0 = s // SUBLANES
                g_s = gh[s:s + 1, :]
                k_s = kh[s:s + 1, :]
                v_s = vh[s:s + 1, :]
                for rb in range(rb0, nrb):
                    bs = slice(rb * SUBLANES, (rb + 1) * SUBLANES)
                    d = gh[bs, :] - g_s
                    if rb == rb0:
                        d = jnp.where(rows8 >= (s - rb0 * SUBLANES), d, NEG)
                    p = qh[bs, :] * k_s * jnp.exp(d)
                    a = jnp.sum(p, axis=-1, keepdims=True)
                    contrib = a * v_s
                    acc[rb] = contrib if acc[rb] is None else acc[rb] + contrib
            o_intra = jnp.concatenate(acc, axis=0) if nrb > 1 else acc[0]
            st = st_ref[h]
            o_h = o_intra + _dot_nt(qe[:, hs], st.astype(BF16))
            st_ref[h] = st * dec[:, hs] + _dot_tn(vb[:, hs], kd[:, hs])
            ms = jnp.mean(o_h * o_h, axis=-1, keepdims=True)
            o_h = o_h * lax.rsqrt(ms + EPS) * ang[:, hs] * out_gate[:, hs]
            oa_ref[rs, hs] = o_h.astype(oa_ref.dtype)

        gt = gt_ref[rs, :] + gbias_ref[...]
        lf = jnp.minimum(gt, 0.0) - jnp.log(1.0 + jnp.exp(-jnp.abs(gt)))
        bcum = _cumsum_rows(tril, lf)
        bv = bv_ref[rs, :]
        bo = bo_ref[rs, :]
        bng = bng_ref[...]
        for h in range(B_HEADS):
            hs = slice(h * B_DH, (h + 1) * B_DH)
            b_col = bcum[:, B_HEADS + h:B_HEADS + h + 1]
            i_col = gt[:, h:h + 1]
            m_prev = m_ref[h][0:1, 0:1]
            lhs = jnp.where(lane == 0, b_col, jnp.where(lane == 1, 1.0, 0.0))
            rhs = jnp.where(lane == 0, 1.0, jnp.where(lane == 1, i_col - b_col, 0.0))
            d_intra = lax.dot_general(lhs, rhs, (((1,), (1,)), ((), ())),
                                      precision=lax.Precision.HIGHEST, preferred_element_type=F32)
            d_intra = jnp.where(causal, d_intra, NEG)
            d_inter = b_col + m_prev
            m_t = jnp.maximum(jnp.max(d_intra, axis=-1, keepdims=True), d_inter)
            w = jnp.exp(d_intra - m_t)
            w_inter = jnp.exp(d_inter - m_t)
            qh = qb_ref[rs, hs]
            kh = kb_ref[rs, hs]
            vh = bv[:, hs]
            qhb = qh.astype(BF16)
            vhb = vh.astype(BF16)
            smat = _dot_nt(qhb, kh.astype(BF16)) * w
            cm = c_ref[h]
            nv = n_ref[h:h + 1, :]
            num = jnp.dot(smat.astype(BF16), vhb, preferred_element_type=F32) + w_inter * jnp.dot(
                qhb, cm.astype(BF16), preferred_element_type=F32)
            den = jnp.sum(smat, axis=-1, keepdims=True) + w_inter * jnp.sum(qh * nv, axis=-1, keepdims=True)
            hh = num / jnp.maximum(jnp.abs(den), jnp.exp(-m_t))
            m_new = m_t[c - 1:c, :]
            b_last = b_col[c - 1:c, :]
            w_state = jnp.exp(b_last - b_col + i_col - m_new)
            decay = jnp.exp(b_last + m_prev - m_new)
            kw = kh * w_state
            c_ref[h] = decay * cm + _dot_tn(kw.astype(BF16), vhb)
            n_ref[h:h + 1, :] = decay * nv + jnp.sum(kw, axis=0, keepdims=True)
            m_ref[h] = jnp.broadcast_to(m_new, (SUBLANES, LANES))
            mu = jnp.mean(hh, axis=-1, keepdims=True)
            xc = hh - mu
            var = jnp.mean(xc * xc, axis=-1, keepdims=True)
            hn = xc * lax.rsqrt(var + EPS) * bng[:, hs]
            hb_ref[rs, hs] = (_sigmoid(bo[:, hs]) * hn).astype(hb_ref.dtype)

    nch = lt // c
    if nch == 1:
        chunk(0)
    else:
        def body(ci, carry):
            chunk(pl.multiple_of(ci * c, c))
            return carry
        lax.fori_loop(0, nch, body, 0)

    @pl.when(step == nsteps - 1)
    def _():
        for h in range(A_HEADS):
            so_ref[0, h] = st_ref[h].T
        co_ref[0] = c_ref[...]
        no_ref[0] = n_ref[0:B_HEADS, :]
        for h in range(B_HEADS):
            mo_ref[0, h:h + 1, :] = m_ref[h][0:1, :]
        cvo_ref[0] = xp_ref[halo - (CONV_W - 1):halo, :]


def _mixer(z, gates, row0, nseq, seqlen, layer, consts, state, out_dtype):
    c = CHUNK if seqlen % CHUNK == 0 else seqlen
    lt = _pick_tile(seqlen, (256, 128, 64, 32)) if c == CHUNK else seqlen
    nsteps = seqlen // lt
    blk0 = row0 // lt
    has_state = state is not None

    def zspec(colblk):
        return pl.BlockSpec((lt, D_MODEL), lambda b, i, cb=colblk: (blk0 + b * nsteps + i, cb))

    def full(shape):
        nd = len(shape)
        return pl.BlockSpec(shape, lambda b, i: (0,) * nd)

    def per_seq(shape):
        nd = len(shape)
        return pl.BlockSpec((1,) + shape, lambda b, i: (b,) + (0,) * nd)

    in_specs = [zspec(COL_AQ), zspec(COL_AF), zspec(COL_AI), zspec(COL_AG), zspec(COL_BU), zspec(COL_BV),
                zspec(COL_BO),
                pl.BlockSpec((lt, LANES), lambda b, i: (blk0 + b * nsteps + i, 0))]
    args = [z] * 7 + [gates]
    for a in consts:
        in_specs.append(full(a.shape))
        args.append(a)
    if has_state:
        for a in state:
            in_specs.append(per_seq(a.shape[1:]))
            args.append(a)

    tok = nseq * seqlen
    out_shape = [
        jax.ShapeDtypeStruct((tok, D_MODEL), out_dtype),
        jax.ShapeDtypeStruct((tok, D_MODEL), out_dtype),
        jax.ShapeDtypeStruct((nseq, A_HEADS, A_DK, A_DV), F32),
        jax.ShapeDtypeStruct((nseq, B_HEADS, B_DH, B_DH), F32),
        jax.ShapeDtypeStruct((nseq, B_HEADS, B_DH), F32),
        jax.ShapeDtypeStruct((nseq, B_HEADS, LANES), F32),
        jax.ShapeDtypeStruct((nseq, CONV_W - 1, D_MODEL), F32),
    ]
    tok_spec = pl.BlockSpec((lt, D_MODEL), lambda b, i: (b * nsteps + i, 0))
    out_specs = [tok_spec, tok_spec,
                 per_seq((A_HEADS, A_DK, A_DV)), per_seq((B_HEADS, B_DH, B_DH)), per_seq((B_HEADS, B_DH)),
                 per_seq((B_HEADS, LANES)), per_seq((CONV_W - 1, D_MODEL))]
    scratch = [
        pltpu.VMEM((A_HEADS, A_DV, A_DK), F32),
        pltpu.VMEM((B_HEADS, B_DH, B_DH), F32),
        pltpu.VMEM((SUBLANES, B_DH), F32),
        pltpu.VMEM((B_HEADS, SUBLANES, LANES), F32),
        pltpu.VMEM((lt + SUBLANES, D_MODEL), F32),
        pltpu.VMEM((lt, D_MODEL), F32),
        pltpu.VMEM((lt, D_MODEL), F32),
    ]
    kern = functools.partial(_mixer_kernel, layer=layer, lt=lt, c=c, has_state=has_state, nsteps=nsteps)
    return pl.pallas_call(
        kern,
        grid=(nseq, nsteps),
        in_specs=in_specs,
        out_specs=out_specs,
        out_shape=out_shape,
        scratch_shapes=scratch,
        compiler_params=pltpu.CompilerParams(
            dimension_semantics=("arbitrary", "arbitrary"), vmem_limit_bytes=VMEM_LIMIT),
        name="mixer_state" if has_state else "mixer_prompt",
    )(*args)


def _layer_norm_rows(r, g, b):
    mu = jnp.mean(r, axis=-1, keepdims=True)
    xc = r - mu
    var = jnp.mean(xc * xc, axis=-1, keepdims=True)
    return xc * lax.rsqrt(var + EPS) * g + b


def _post_kernel(*refs, with_router):
    oa_ref, hb_ref, ga_ref, gb_ref, x_ref, wpa_ref, wpb_ref, wo_ref, g1_ref, b1_ref = refs[:10]
    if with_router:
        wr_ref, br_ref, out_ref, gate_ref = refs[10:]
    else:
        (out_ref,) = refs[10:]
    ya = jnp.dot(oa_ref[...].astype(BF16), wpa_ref[...], preferred_element_type=F32)
    yb = jnp.dot(hb_ref[...].astype(BF16), wpb_ref[...], preferred_element_type=F32)
    merged = _sigmoid(ga_ref[...]) * ya + _sigmoid(gb_ref[...]) * yb
    o = jnp.dot(merged.astype(BF16), wo_ref[...], preferred_element_type=F32)
    x1 = _layer_norm_rows(ALPHA * x_ref[...] + o, g1_ref[...], b1_ref[...])
    out_ref[...] = x1
    if with_router:
        logits = jnp.dot(x1, wr_ref[...], precision=lax.Precision.HIGHEST,
                         preferred_element_type=F32) + br_ref[...]
        lane = lax.broadcasted_iota(jnp.int32, logits.shape, 1)
        m1 = jnp.max(logits, axis=-1, keepdims=True)
        i1 = jnp.min(jnp.where(logits == m1, lane, LANES), axis=-1, keepdims=True)
        rest = jnp.where(lane == i1, NEG * 2.0, logits)
        m2 = jnp.max(rest, axis=-1, keepdims=True)
        i2 = jnp.min(jnp.where(rest == m2, lane, LANES), axis=-1, keepdims=True)
        e2 = jnp.exp(m2 - m1)
        w1 = 1.0 / (1.0 + e2)
        gate_ref[...] = jnp.where(lane == i1, w1, 0.0) + jnp.where(lane == i2, e2 * w1, 0.0)


def _post(oa, hb, z, x, wpa, wpb, wo, g1, b1, router):
    t = x.shape[0]
    tm = _pick_tile(t, (512, 256, 128, 64, 32, 16, 8))
    tok = pl.BlockSpec((tm, D_MODEL), lambda i: (i, 0))
    wspec = pl.BlockSpec((D_MODEL, D_MODEL), lambda i: (0, 0))
    vec = pl.BlockSpec((1, D_MODEL), lambda i: (0, 0))
    in_specs = [tok, tok,
                pl.BlockSpec((tm, D_MODEL), lambda i: (i, COL_GA)),
                pl.BlockSpec((tm, D_MODEL), lambda i: (i, COL_GB)),
                tok, wspec, wspec, wspec, vec, vec]
    args = [oa, hb, z, z, x, wpa, wpb, wo, g1, b1]
    out_shape = [jax.ShapeDtypeStruct((t, D_MODEL), F32)]
    out_specs = [tok]
    if router is not None:
        in_specs += [pl.BlockSpec((D_MODEL, LANES), lambda i: (0, 0)), pl.BlockSpec((1, LANES), lambda i: (0, 0))]
        args += list(router)
        out_shape.append(jax.ShapeDtypeStruct((t, LANES), F32))
        out_specs.append(pl.BlockSpec((tm, LANES), lambda i: (i, 0)))
    res = pl.pallas_call(
        functools.partial(_post_kernel, with_router=router is not None),
        grid=(t // tm,),
        in_specs=in_specs,
        out_specs=out_specs,
        out_shape=out_shape,
        compiler_params=pltpu.CompilerParams(dimension_semantics=("arbitrary",), vmem_limit_bytes=VMEM_LIMIT),
        name="post_router" if router is not None else "post",
    )(*args)
    return res if router is not None else (res[0], None)


def _ffn_kernel(*refs, gated, n_e, n_f):
    if gated:
        x_ref, gate_ref, wg_ref, wu_ref, wd_ref, g2_ref, b2_ref, out_ref, acc_ref, xb_ref = refs
    else:
        x_ref, wg_ref, wu_ref, wd_ref, g2_ref, b2_ref, out_ref, acc_ref, xb_ref = refs
    e = pl.program_id(1)
    j = pl.program_id(2)

    @pl.when((e == 0) & (j == 0))
    def _():
        acc_ref[...] = jnp.zeros_like(acc_ref)
        xb_ref[...] = x_ref[...].astype(BF16)

    xb = xb_ref[...]
    hg = jnp.dot(xb, wg_ref[...], preferred_element_type=F32)
    hu = jnp.dot(xb, wu_ref[...], preferred_element_type=F32)
    hact = (hg * _sigmoid(hg) * hu).astype(BF16)
    y = jnp.dot(hact, wd_ref[...], preferred_element_type=F32)
    if gated:
        gate = gate_ref[...]
        lane = lax.broadcasted_iota(jnp.int32, gate.shape, 1)
        y = y * jnp.sum(jnp.where(lane == e, gate, 0.0), axis=-1, keepdims=True)
    acc_ref[...] += y

    @pl.when((e == n_e - 1) & (j == n_f - 1))
    def _():
        out_ref[...] = _layer_norm_rows(ALPHA * x_ref[...] + acc_ref[...], g2_ref[...], b2_ref[...])


def _ffn(x, gate, wg, wu, wd, g2, b2):
    t = x.shape[0]
    n_e = wg.shape[0]
    tm = _pick_tile(t, (512, 256, 128, 64, 32, 16, 8))
    tf = D_FF // 2
    n_f = D_FF // tf
    gated = gate is not None
    tok = pl.BlockSpec((tm, D_MODEL), lambda i, e, j: (i, 0))
    vec = pl.BlockSpec((1, D_MODEL), lambda i, e, j: (0, 0))
    in_specs = [tok]
    args = [x]
    if gated:
        in_specs.append(pl.BlockSpec((tm, LANES), lambda i, e, j: (i, 0)))
        args.append(gate)
    in_specs += [
        pl.BlockSpec((None, D_MODEL, tf), lambda i, e, j: (e, 0, j)),
        pl.BlockSpec((None, D_MODEL, tf), lambda i, e, j: (e, 0, j)),
        pl.BlockSpec((None, tf, D_MODEL), lambda i, e, j: (e, j, 0)),
        vec, vec]
    args += [wg, wu, wd, g2, b2]
    return pl.pallas_call(
        functools.partial(_ffn_kernel, gated=gated, n_e=n_e, n_f=n_f),
        grid=(t // tm, n_e, n_f),
        in_specs=in_specs,
        out_specs=tok,
        out_shape=jax.ShapeDtypeStruct((t, D_MODEL), F32),
        scratch_shapes=[pltpu.VMEM((tm, D_MODEL), F32), pltpu.VMEM((tm, D_MODEL), BF16)],
        compiler_params=pltpu.CompilerParams(
            dimension_semantics=("arbitrary", "arbitrary", "arbitrary"), vmem_limit_bytes=VMEM_LIMIT),
        name="ffn_moe" if gated else "ffn_dense",
    )(*args)


def _block_diag_dense(w_blk):
    per = B_DH // QK_BLOCK
    w = w_blk.reshape(B_HEADS, per, QK_BLOCK, QK_BLOCK)
    eye = jnp.eye(per, dtype=w.dtype)
    dense = eye[None, :, None, :, None] * w[:, :, :, None, :]
    return dense.reshape(B_HEADS, B_DH, B_DH)


def kernel(x_prompt, x_sample, state_hgrn, state_mlstm_c, state_mlstm_n, state_mlstm_m, state_conv, w_in, lb_param, a_norm_g, w_pa, conv_w, conv_b, w_q_blk, w_k_blk, b_ig, b_fg, b_norm_g, w_pb, w_o, ln1_g, ln1_b, ln2_g, ln2_b, w_ff_gate, w_ff_up, w_ff_down, w_router, b_router, w_ex_gate, w_ex_up, w_ex_down):
    bp, seq, _ = x_prompt.shape
    bs, dseq, _ = x_sample.shape
    depth = w_in.shape[0]
    n_prompt = bp * seq
    x = jnp.concatenate([x_prompt.reshape(n_prompt, D_MODEL), x_sample.reshape(bs * dseq, D_MODEL)], axis=0)

    gate_lo = 7 * D_MODEL
    gate_hi = gate_lo + 2 * B_HEADS
    new_p = [[] for _ in range(5)]
    new_s = [[] for _ in range(5)]
    for l in range(depth):
        wl = w_in[l]
        w_main = jnp.concatenate([wl[:, :gate_lo], wl[:, gate_hi:]], axis=1).astype(BF16)
        w_gate = jnp.pad(wl[:, gate_lo:gate_hi], ((0, 0), (0, LANES - 2 * B_HEADS))).astype(BF16)
        z, gates = _proj_in(x, w_main, w_gate)

        gbias = jnp.pad(jnp.concatenate([b_ig[l], b_fg[l]]), (0, LANES - 2 * B_HEADS)).reshape(1, LANES)
        consts = [lb_param.astype(F32), a_norm_g[l].reshape(1, D_MODEL), conv_w[l], conv_b[l].reshape(1, D_MODEL),
                  _block_diag_dense(w_q_blk[l]).astype(BF16), _block_diag_dense(w_k_blk[l]).astype(BF16),
                  gbias, b_norm_g[l].reshape(1, D_MODEL)]
        m_in = jnp.broadcast_to(state_mlstm_m[l][:, :, None], (bs, B_HEADS, LANES))
        state = [state_hgrn[l], state_mlstm_c[l], state_mlstm_n[l], m_in, state_conv[l]]
        outs_p = _mixer(z, gates, 0, bp, seq, l, consts, None, F32)
        outs_s = _mixer(z, gates, n_prompt, bs, dseq, l, consts, state, F32)
        oa = jnp.concatenate([outs_p[0], outs_s[0]], axis=0)
        hb = jnp.concatenate([outs_p[1], outs_s[1]], axis=0)
        for acc, outs in ((new_p, outs_p), (new_s, outs_s)):
            acc[0].append(outs[2])
            acc[1].append(outs[3])
            acc[2].append(outs[4])
            acc[3].append(outs[5][:, :, 0])
            acc[4].append(outs[6])

        j = l // 2
        if l % 2 == 0:
            router = None
        else:
            wr = jnp.pad(w_router[j], ((0, 0), (0, LANES - N_EXPERTS)))
            br = jnp.concatenate([b_router[j].astype(F32), jnp.full((LANES - N_EXPERTS,), NEG, F32)]).reshape(1, LANES)
            router = (wr, br)
        x1, gate = _post(oa, hb, z, x, w_pa[l].astype(BF16), w_pb[l].astype(BF16), w_o[l].astype(BF16),
                         ln1_g[l].reshape(1, D_MODEL), ln1_b[l].reshape(1, D_MODEL), router)
        g2 = ln2_g[l].reshape(1, D_MODEL)
        b2 = ln2_b[l].reshape(1, D_MODEL)
        if l % 2 == 0:
            x = _ffn(x1, None, w_ff_gate[j][None].astype(BF16), w_ff_up[j][None].astype(BF16),
                     w_ff_down[j][None].astype(BF16), g2, b2)
        else:
            x = _ffn(x1, gate, w_ex_gate[j].astype(BF16), w_ex_up[j].astype(BF16), w_ex_down[j].astype(BF16), g2, b2)

    y_p = x[:n_prompt].reshape(bp, seq, D_MODEL)
    y_s = x[n_prompt:].reshape(bs, dseq, D_MODEL)
    hgrn_p, c_p, n_p, m_p, conv_p = (jnp.stack(a) for a in new_p)
    hgrn_s, c_s, n_s, m_s, conv_s = (jnp.stack(a) for a in new_s)
    return (y_p, y_s, hgrn_p, hgrn_s, c_p, c_s, n_p, n_s, m_p, m_s, conv_p, conv_s)
```

```python
import functools

import numpy as np
import jax
import jax.numpy as jnp
from jax import lax
from jax.experimental import pallas as pl
from jax.experimental.pallas import tpu as pltpu

F32 = jnp.float32
BF16 = jnp.bfloat16

D_MODEL = 1024
A_HEADS = 8
A_DK = 128
A_DV = 128
B_HEADS = 4
B_DH = 256
QK_BLOCK = 4
CONV_W = 4
CHUNK = 32
D_FF = 2816
N_EXPERTS = 8
DEPTH = 4
ALPHA = (2 * DEPTH) ** 0.25
EPS = 1e-5
NEG = -1e30
TINY = 1e-30
LOG2E = 1.4426950408889634
MIXER_CHUNK = 64

LANES = 128
SUBLANES = 8
VMEM_LIMIT = 56 * 1024 * 1024

COL_AQ, COL_AF, COL_AI, COL_AG, COL_BU, COL_BV, COL_BO, COL_GA, COL_GB = range(9)
N_MAIN_BLOCKS = 9


def _pick_tile(n, prefs):
    for p in prefs:
        if n % p == 0:
            return p
    return n


def _sigmoid(x):
    return 1.0 / (1.0 + jnp.exp(-x))


def _split3(x):
    hi = x.astype(BF16)
    r1 = x - hi.astype(F32)
    mid = r1.astype(BF16)
    lo = (r1 - mid.astype(F32)).astype(BF16)
    return hi, mid, lo


def _cumsum_rows(tril_bf16, x):
    hi, mid, lo = _split3(x)
    acc = jnp.dot(tril_bf16, lo, preferred_element_type=F32)
    acc = acc + jnp.dot(tril_bf16, mid, preferred_element_type=F32)
    return acc + jnp.dot(tril_bf16, hi, preferred_element_type=F32)


def _dot_nt(a, b):
    return lax.dot_general(a, b, (((1,), (1,)), ((), ())), preferred_element_type=F32)


def _dot_tn(a, b):
    return lax.dot_general(a, b, (((0,), (0,)), ((), ())), preferred_element_type=F32)


def _proj_in_kernel(x_ref, w_ref, wg_ref, z_ref, g_ref, xb_ref):
    @pl.when(pl.program_id(1) == 0)
    def _():
        xb = x_ref[...].astype(BF16)
        xb_ref[...] = xb
        g_ref[...] = jnp.dot(xb, wg_ref[...], preferred_element_type=F32)

    z_ref[...] = jnp.dot(xb_ref[...], w_ref[...], preferred_element_type=F32)


def _proj_in(x, w_main, w_gate):
    t = x.shape[0]
    tm = _pick_tile(t, (1024, 512, 256, 128, 64, 32, 16, 8))
    return pl.pallas_call(
        _proj_in_kernel,
        grid=(t // tm, N_MAIN_BLOCKS),
        in_specs=[
            pl.BlockSpec((tm, D_MODEL), lambda i, j: (i, 0)),
            pl.BlockSpec((D_MODEL, D_MODEL), lambda i, j: (0, j)),
            pl.BlockSpec((D_MODEL, LANES), lambda i, j: (0, 0)),
        ],
        out_specs=[
            pl.BlockSpec((tm, D_MODEL), lambda i, j: (i, j)),
            pl.BlockSpec((tm, LANES), lambda i, j: (i, 0)),
        ],
        out_shape=[
            jax.ShapeDtypeStruct((t, N_MAIN_BLOCKS * D_MODEL), F32),
            jax.ShapeDtypeStruct((t, LANES), F32),
        ],
        scratch_shapes=[pltpu.VMEM((tm, D_MODEL), BF16)],
        compiler_params=pltpu.CompilerParams(
            dimension_semantics=("arbitrary", "arbitrary"), vmem_limit_bytes=VMEM_LIMIT),
        name="proj_in",
    )(x, w_main, w_gate)


def _level_tables(c):
    levels = c.bit_length() - 1
    assert 1 << levels == c
    blocks = []
    for l in range(levels):
        a = np.zeros((c, c), np.float32)
        for r in range(c):
            right_start = ((r >> (l + 1)) << (l + 1)) + (1 << l)
            if (r >> l) & 1:
                a[r, right_start:r + 1] = 1.0
            else:
                a[r, r + 1:right_start] = 1.0
        blocks.append(a)
    blocks.append(np.tril(np.ones((c, c), np.float32)))
    blocks.append(np.triu(np.ones((c, c), np.float32), k=1))
    lv = np.full((c, c), -1, np.int32)
    for t in range(c):
        for s_ in range(t):
            lv[t, s_] = (t ^ s_).bit_length() - 1
        lv[t, t] = levels
    return np.concatenate(blocks, axis=0), lv, levels


def _mixer_kernel(*refs, layer, lt, c, levels, has_state, n_alias, nsteps):
    (aq_ref, af_ref, ai_ref, ag_ref, bu_ref, bv_ref, bo_ref, gt_ref,
     lbp_ref, ang_ref, cw_ref, cb_ref, wq_ref, wk_ref, gbias_ref, bng_ref,
     atab_ref, tril_ref, lv_ref) = refs[:19]
    pos = 19
    if has_state:
        s0_ref, c0_ref, n0_ref, m0_ref, cv0_ref = refs[pos:pos + 5]
        pos += 5
    pos += n_alias
    oa_ref, hb_ref, so_ref, co_ref, no_ref, mo_ref, cvo_ref = refs[pos:pos + 7]
    pos += 7
    st_ref, c_ref, n_ref, m_ref, xp_ref, qb_ref, kb_ref = refs[pos:]

    step = pl.program_id(1)
    halo = SUBLANES

    @pl.when(step == 0)
    def _():
        xp_ref[0:halo, :] = jnp.zeros((halo, D_MODEL), F32)
        if has_state:
            for h in range(A_HEADS):
                st_ref[h] = s0_ref[0, h].T
            c_ref[...] = c0_ref[0]
            n_ref[0:B_HEADS, :] = n0_ref[0]
            m0 = m0_ref[0]
            for h in range(B_HEADS):
                m_ref[h] = jnp.broadcast_to(m0[h:h + 1, :], (SUBLANES, LANES))
            xp_ref[halo - (CONV_W - 1):halo, :] = cv0_ref[0]
        else:
            st_ref[...] = jnp.zeros_like(st_ref)
            c_ref[...] = jnp.zeros_like(c_ref)
            n_ref[...] = jnp.zeros_like(n_ref)
            m_ref[...] = jnp.zeros_like(m_ref)

    xp_ref[halo:halo + lt, :] = bu_ref[...]
    y = cb_ref[...]
    for j in range(CONV_W):
        y = y + xp_ref[halo - (CONV_W - 1) + j:halo - (CONV_W - 1) + j + lt, :] * cw_ref[j:j + 1, :]
    tail = xp_ref[halo + lt - (CONV_W - 1):halo + lt, :]
    xp_ref[halo - (CONV_W - 1):halo, :] = tail
    u = (y * _sigmoid(y)).astype(BF16)
    for g in range(B_HEADS):
        sl = slice(g * B_DH, (g + 1) * B_DH)
        qb_ref[:, sl] = jnp.dot(u[:, sl], wq_ref[g], preferred_element_type=F32)
        kb_ref[:, sl] = jnp.dot(u[:, sl], wk_ref[g], preferred_element_type=F32) * (B_DH ** -0.5)

    lbp = lbp_ref[...]
    lbe = jnp.exp(lbp - jnp.max(lbp, axis=0, keepdims=True))
    lbs = lbe / jnp.sum(lbe, axis=0, keepdims=True)
    lb = jnp.zeros((1, D_MODEL), F32)
    for r in range(1, layer + 1):
        lb = lb + lbs[r:r + 1, :]
    one_m_lb = 1.0 - lb

    lv = lv_ref[...]
    tril = tril_ref[...]
    causal = lv >= 0
    masks = [lv == l for l in range(levels + 1)]
    lane = lax.broadcasted_iota(jnp.int32, (c, LANES), 1)

    def chunk(r0):
        rs = pl.ds(r0, c)
        fp = af_ref[rs, :]
        e = jnp.exp(-jnp.abs(fp))
        r = 1.0 / (1.0 + e)
        er = e * r
        sig_p = jnp.where(fp >= 0, r, er)
        sig_n = jnp.where(fp >= 0, er, r)
        f_gate = lb + one_m_lb * sig_p
        lf2 = jnp.log(jnp.maximum(f_gate, TINY)) * LOG2E
        key = one_m_lb * sig_n
        aq = aq_ref[rs, :]
        q = aq * _sigmoid(aq)
        vb = ai_ref[rs, :].astype(BF16)
        hi, mid, lo = _split3(lf2)
        if c >= 2 * SUBLANES:
            x3 = jnp.concatenate([lo, mid, hi], axis=0)
            eg = jnp.dot(atab_ref[...], x3, preferred_element_type=F32)
        else:
            atab = atab_ref[...]
            eg = jnp.dot(atab, lo, preferred_element_type=F32)
            eg = eg + jnp.dot(atab, mid, preferred_element_type=F32)
            eg = eg + jnp.dot(atab, hi, preferred_element_type=F32)

        gt = gt_ref[rs, :] + gbias_ref[...]
        lf = jnp.minimum(gt, 0.0) - jnp.log(1.0 + jnp.exp(-jnp.abs(gt)))
        bcum = _cumsum_rows(tril, lf)
        qm = qb_ref[rs, :]
        km = kb_ref[rs, :]
        qmb = qm.astype(BF16)
        kmb = km.astype(BF16)
        vmb = bv_ref[rs, :].astype(BF16)

        b_col, i_col, m_prev, d_intra, qk, qc, cm, nv = [], [], [], [], [], [], [], []
        for h in range(B_HEADS):
            hs = slice(h * B_DH, (h + 1) * B_DH)
            b_col.append(bcum[:, B_HEADS + h:B_HEADS + h + 1])
            i_col.append(gt[:, h:h + 1])
            m_prev.append(m_ref[h][0:1, 0:1])
            lhs = jnp.where(lane == 0, b_col[h], jnp.where(lane == 1, 1.0, 0.0))
            rhs = jnp.where(lane == 0, 1.0, jnp.where(lane == 1, i_col[h] - b_col[h], 0.0))
            d_intra.append(lax.dot_general(lhs, rhs, (((1,), (1,)), ((), ())),
                                           precision=lax.Precision.HIGHEST, preferred_element_type=F32))
            qk.append(_dot_nt(qmb[:, hs], kmb[:, hs]))
            cm.append(c_ref[h])
            nv.append(n_ref[h:h + 1, :])
            qc.append(jnp.dot(qmb[:, hs], cm[h].astype(BF16), preferred_element_type=F32))

        w_all = jnp.exp2(eg)
        wg = w_all[levels * c:(levels + 1) * c, :]
        wk = w_all[(levels + 1) * c:(levels + 2) * c, :]
        qe = (q * wg).astype(BF16)
        kd = (key * wk).astype(BF16)
        lvl = []
        for l in range(levels):
            wl = w_all[l * c:(l + 1) * c, :]
            qw = (q * wl).astype(BF16)
            kw = (key * wl).astype(BF16)
            lvl.append([_dot_nt(qw[:, h * A_DK:(h + 1) * A_DK], kw[:, h * A_DK:(h + 1) * A_DK])
                        for h in range(A_HEADS)])
        qb16 = q.astype(BF16)
        kb16 = key.astype(BF16)
        lvl.append([_dot_nt(qb16[:, h * A_DK:(h + 1) * A_DK], kb16[:, h * A_DK:(h + 1) * A_DK])
                    for h in range(A_HEADS)])
        st = [st_ref[h] for h in range(A_HEADS)]
        inter = [_dot_nt(qe[:, h * A_DK:(h + 1) * A_DK], st[h].astype(BF16)) for h in range(A_HEADS)]

        att = []
        for h in range(A_HEADS):
            a = jnp.where(masks[levels], lvl[levels][h], 0.0)
            for l in range(levels):
                a = jnp.where(masks[l], lvl[l][h], a)
            att.append(a.astype(BF16))
        smat, w_inter, m_t = [], [], []
        for h in range(B_HEADS):
            d = jnp.where(causal, d_intra[h], NEG)
            d_inter = b_col[h] + m_prev[h]
            m_t.append(jnp.maximum(jnp.max(d, axis=-1, keepdims=True), d_inter))
            smat.append(qk[h] * jnp.exp(d - m_t[h]))
            w_inter.append(jnp.exp(d_inter - m_t[h]))

        o_intra = [jnp.dot(att[h], vb[:, h * A_DV:(h + 1) * A_DV], preferred_element_type=F32)
                   for h in range(A_HEADS)]
        s_upd = [_dot_tn(vb[:, h * A_DV:(h + 1) * A_DV], kd[:, h * A_DK:(h + 1) * A_DK]) for h in range(A_HEADS)]
        num_a, c_upd, kws, m_new, decay = [], [], [], [], []
        for h in range(B_HEADS):
            hs = slice(h * B_DH, (h + 1) * B_DH)
            num_a.append(jnp.dot(smat[h].astype(BF16), vmb[:, hs], preferred_element_type=F32))
            m_new.append(m_t[h][c - 1:c, :])
            b_last = b_col[h][c - 1:c, :]
            w_state = jnp.exp(b_last - b_col[h] + i_col[h] - m_new[h])
            decay.append(jnp.exp(b_last + m_prev[h] - m_new[h]))
            kws.append(km[:, hs] * w_state)
            c_upd.append(_dot_tn(kws[h].astype(BF16), vmb[:, hs]))

        ag = ag_ref[rs, :]
        out_gate = ag * _sigmoid(ag)
        ang = ang_ref[...]
        for h in range(A_HEADS):
            hs = slice(h * A_DK, (h + 1) * A_DK)
            o_h = o_intra[h] + inter[h]
            st_ref[h] = st[h] * wg[c - 1:c, hs] + s_upd[h]
            ms = jnp.mean(o_h * o_h, axis=-1, keepdims=True)
            o_h = o_h * lax.rsqrt(ms + EPS) * ang[:, hs] * out_gate[:, hs]
            oa_ref[rs, hs] = o_h.astype(oa_ref.dtype)
        bo = bo_ref[rs, :]
        bng = bng_ref[...]
        for h in range(B_HEADS):
            hs = slice(h * B_DH, (h + 1) * B_DH)
            num = num_a[h] + w_inter[h] * qc[h]
            den = jnp.sum(smat[h], axis=-1, keepdims=True) + w_inter[h] * jnp.sum(
                qm[:, hs] * nv[h], axis=-1, keepdims=True)
            hh = num / jnp.maximum(jnp.abs(den), jnp.exp(-m_t[h]))
            c_ref[h] = decay[h] * cm[h] + c_upd[h]
            n_ref[h:h + 1, :] = decay[h] * nv[h] + jnp.sum(kws[h], axis=0, keepdims=True)
            m_ref[h] = jnp.broadcast_to(m_new[h], (SUBLANES, LANES))
            mu = jnp.mean(hh, axis=-1, keepdims=True)
            xc = hh - mu
            var = jnp.mean(xc * xc, axis=-1, keepdims=True)
            hn = xc * lax.rsqrt(var + EPS) * bng[:, hs]
            hb_ref[rs, hs] = (_sigmoid(bo[:, hs]) * hn).astype(hb_ref.dtype)

    nch = lt // c
    if nch == 1:
        chunk(0)
    else:
        def body(ci, carry):
            chunk(pl.multiple_of(ci * c, c))
            return carry
        lax.fori_loop(0, nch, body, 0)

    @pl.when(step == nsteps - 1)
    def _():
        for h in range(A_HEADS):
            so_ref[0, h] = st_ref[h].T
        co_ref[0] = c_ref[...]
        no_ref[0] = n_ref[0:B_HEADS, :]
        for h in range(B_HEADS):
            mo_ref[0, h:h + 1, :] = m_ref[h][0:1, 0:1]
        cvo_ref[0] = xp_ref[halo - (CONV_W - 1):halo, :]


def _state_shapes(depth, nseq):
    return [(depth, nseq, A_HEADS, A_DK, A_DV), (depth, nseq, B_HEADS, B_DH, B_DH), (depth, nseq, B_HEADS, B_DH),
            (depth, nseq, B_HEADS, 1), (depth, nseq, CONV_W - 1, D_MODEL)]


def _mixer(z, gates, row0, nseq, seqlen, layer, depth, consts, state, prev_tok, prev_state):
    c = min(seqlen, MIXER_CHUNK) if seqlen % CHUNK == 0 else seqlen
    lt = _pick_tile(seqlen, (256, 128, 64, 32)) if seqlen % CHUNK == 0 else seqlen
    assert lt % c == 0
    nsteps = seqlen // lt
    blk0 = row0 // lt
    has_state = state is not None
    a_all, lv, levels = _level_tables(c)
    if c >= 2 * SUBLANES:
        a_all = np.concatenate([a_all] * 3, axis=1)
    tables = [jnp.asarray(a_all, BF16), jnp.asarray(np.tril(np.ones((c, c), np.float32)), BF16), jnp.asarray(lv)]

    def zspec(colblk):
        return pl.BlockSpec((lt, D_MODEL), lambda b, i, cb=colblk: (blk0 + b * nsteps + i, cb))

    def full(shape):
        nd = len(shape)
        return pl.BlockSpec(shape, lambda b, i: (0,) * nd)

    def per_seq(shape):
        nd = len(shape) - 2
        return pl.BlockSpec((None, 1) + tuple(shape[2:]), lambda b, i: (layer, b) + (0,) * nd)

    in_specs = [zspec(COL_AQ), zspec(COL_AF), zspec(COL_AI), zspec(COL_AG), zspec(COL_BU), zspec(COL_BV),
                zspec(COL_BO),
                pl.BlockSpec((lt, LANES), lambda b, i: (blk0 + b * nsteps + i, 0))]
    args = [z] * 7 + [gates]
    for a in list(consts) + tables:
        in_specs.append(full(a.shape))
        args.append(a)
    if has_state:
        for a in state:
            in_specs.append(per_seq(a.shape))
            args.append(a)
    aliases = {}
    n_alias = 0
    for first_out, prev in ((0, prev_tok), (2, prev_state)):
        if prev is not None:
            for k, a in enumerate(prev):
                aliases[len(args)] = first_out + k
                in_specs.append(pl.BlockSpec(memory_space=pl.ANY))
                args.append(a)
            n_alias += len(prev)

    tok = z.shape[0]
    st_shapes = _state_shapes(depth, nseq)
    out_shape = [jax.ShapeDtypeStruct((tok, D_MODEL), F32), jax.ShapeDtypeStruct((tok, D_MODEL), F32)]
    out_shape += [jax.ShapeDtypeStruct(s, F32) for s in st_shapes]
    tok_spec = pl.BlockSpec((lt, D_MODEL), lambda b, i: (blk0 + b * nsteps + i, 0))
    out_specs = [tok_spec, tok_spec] + [per_seq(s) for s in st_shapes]
    scratch = [
        pltpu.VMEM((A_HEADS, A_DV, A_DK), F32),
        pltpu.VMEM((B_HEADS, B_DH, B_DH), F32),
        pltpu.VMEM((SUBLANES, B_DH), F32),
        pltpu.VMEM((B_HEADS, SUBLANES, LANES), F32),
        pltpu.VMEM((lt + SUBLANES, D_MODEL), F32),
        pltpu.VMEM((lt, D_MODEL), F32),
        pltpu.VMEM((lt, D_MODEL), F32),
    ]
    kern = functools.partial(_mixer_kernel, layer=layer, lt=lt, c=c, levels=levels, has_state=has_state,
                             n_alias=n_alias, nsteps=nsteps)
    return pl.pallas_call(
        kern,
        grid=(nseq, nsteps),
        in_specs=in_specs,
        out_specs=out_specs,
        out_shape=out_shape,
        scratch_shapes=scratch,
        input_output_aliases=aliases,
        compiler_params=pltpu.CompilerParams(
            dimension_semantics=("arbitrary", "arbitrary"), vmem_limit_bytes=VMEM_LIMIT),
        name="mixer_state" if has_state else "mixer_prompt",
    )(*args)


def _layer_norm_rows(r, g, b):
    mu = jnp.mean(r, axis=-1, keepdims=True)
    xc = r - mu
    var = jnp.mean(xc * xc, axis=-1, keepdims=True)
    return xc * lax.rsqrt(var + EPS) * g + b


def _post_kernel(*refs, with_router):
    oa_ref, hb_ref, ga_ref, gb_ref, x_ref, wpa_ref, wpb_ref, wo_ref, g1_ref, b1_ref = refs[:10]
    if with_router:
        wr_ref, br_ref, out_ref, gate_ref = refs[10:]
    else:
        (out_ref,) = refs[10:]
    ya = jnp.dot(oa_ref[...].astype(BF16), wpa_ref[...], preferred_element_type=F32)
    yb = jnp.dot(hb_ref[...].astype(BF16), wpb_ref[...], preferred_element_type=F32)
    merged = _sigmoid(ga_ref[...]) * ya + _sigmoid(gb_ref[...]) * yb
    o = jnp.dot(merged.astype(BF16), wo_ref[...], preferred_element_type=F32)
    x1 = _layer_norm_rows(ALPHA * x_ref[...] + o, g1_ref[...], b1_ref[...])
    out_ref[...] = x1
    if with_router:
        logits = jnp.dot(x1, wr_ref[...], precision=lax.Precision.HIGHEST,
                         preferred_element_type=F32) + br_ref[...]
        lane = lax.broadcasted_iota(jnp.int32, logits.shape, 1)
        m1 = jnp.max(logits, axis=-1, keepdims=True)
        i1 = jnp.min(jnp.where(logits == m1, lane, LANES), axis=-1, keepdims=True)
        rest = jnp.where(lane == i1, NEG * 2.0, logits)
        m2 = jnp.max(rest, axis=-1, keepdims=True)
        i2 = jnp.min(jnp.where(rest == m2, lane, LANES), axis=-1, keepdims=True)
        e2 = jnp.exp(m2 - m1)
        w1 = 1.0 / (1.0 + e2)
        gate_ref[...] = jnp.where(lane == i1, w1, 0.0) + jnp.where(lane == i2, e2 * w1, 0.0)


def _post(oa, hb, z, x, wpa, wpb, wo, g1, b1, router):
    t = x.shape[0]
    tm = _pick_tile(t, (512, 256, 128, 64, 32, 16, 8))
    tok = pl.BlockSpec((tm, D_MODEL), lambda i: (i, 0))
    wspec = pl.BlockSpec((D_MODEL, D_MODEL), lambda i: (0, 0))
    vec = pl.BlockSpec((1, D_MODEL), lambda i: (0, 0))
    in_specs = [tok, tok,
                pl.BlockSpec((tm, D_MODEL), lambda i: (i, COL_GA)),
                pl.BlockSpec((tm, D_MODEL), lambda i: (i, COL_GB)),
                tok, wspec, wspec, wspec, vec, vec]
    args = [oa, hb, z, z, x, wpa, wpb, wo, g1, b1]
    out_shape = [jax.ShapeDtypeStruct((t, D_MODEL), F32)]
    out_specs = [tok]
    if router is not None:
        in_specs += [pl.BlockSpec((D_MODEL, LANES), lambda i: (0, 0)), pl.BlockSpec((1, LANES), lambda i: (0, 0))]
        args += list(router)
        out_shape.append(jax.ShapeDtypeStruct((t, LANES), F32))
        out_specs.append(pl.BlockSpec((tm, LANES), lambda i: (i, 0)))
    res = pl.pallas_call(
        functools.partial(_post_kernel, with_router=router is not None),
        grid=(t // tm,),
        in_specs=in_specs,
        out_specs=out_specs,
        out_shape=out_shape,
        compiler_params=pltpu.CompilerParams(dimension_semantics=("arbitrary",), vmem_limit_bytes=VMEM_LIMIT),
        name="post_router" if router is not None else "post",
    )(*args)
    return res if router is not None else (res[0], None)


def _ffn_kernel(*refs, gated, n_e, n_f):
    if gated:
        x_ref, gate_ref, wg_ref, wu_ref, wd_ref, g2_ref, b2_ref, out_ref, acc_ref, xb_ref = refs
    else:
        x_ref, wg_ref, wu_ref, wd_ref, g2_ref, b2_ref, out_ref, acc_ref, xb_ref = refs
    e = pl.program_id(1)
    j = pl.program_id(2)

    @pl.when((e == 0) & (j == 0))
    def _():
        acc_ref[...] = jnp.zeros_like(acc_ref)
        xb_ref[...] = x_ref[...].astype(BF16)

    xb = xb_ref[...]
    hg = jnp.dot(xb, wg_ref[...], preferred_element_type=F32)
    hu = jnp.dot(xb, wu_ref[...], preferred_element_type=F32)
    hact = (hg * _sigmoid(hg) * hu).astype(BF16)
    y = jnp.dot(hact, wd_ref[...], preferred_element_type=F32)
    if gated:
        gate = gate_ref[...]
        lane = lax.broadcasted_iota(jnp.int32, gate.shape, 1)
        y = y * jnp.sum(jnp.where(lane == e, gate, 0.0), axis=-1, keepdims=True)
    acc_ref[...] += y

    @pl.when((e == n_e - 1) & (j == n_f - 1))
    def _():
        out_ref[...] = _layer_norm_rows(ALPHA * x_ref[...] + acc_ref[...], g2_ref[...], b2_ref[...])


def _ffn(x, gate, wg, wu, wd, g2, b2):
    t = x.shape[0]
    n_e = wg.shape[0]
    tm = _pick_tile(t, (512, 256, 128, 64, 32, 16, 8))
    tf = D_FF // 2
    n_f = D_FF // tf
    gated = gate is not None
    tok = pl.BlockSpec((tm, D_MODEL), lambda i, e, j: (i, 0))
    vec = pl.BlockSpec((1, D_MODEL), lambda i, e, j: (0, 0))
    in_specs = [tok]
    args = [x]
    if gated:
        in_specs.append(pl.BlockSpec((tm, LANES), lambda i, e, j: (i, 0)))
        args.append(gate)
    in_specs += [
        pl.BlockSpec((None, D_MODEL, tf), lambda i, e, j: (e, 0, j)),
        pl.BlockSpec((None, D_MODEL, tf), lambda i, e, j: (e, 0, j)),
        pl.BlockSpec((None, tf, D_MODEL), lambda i, e, j: (e, j, 0)),
        vec, vec]
    args += [wg, wu, wd, g2, b2]
    return pl.pallas_call(
        functools.partial(_ffn_kernel, gated=gated, n_e=n_e, n_f=n_f),
        grid=(t // tm, n_e, n_f),
        in_specs=in_specs,
        out_specs=tok,
        out_shape=jax.ShapeDtypeStruct((t, D_MODEL), F32),
        scratch_shapes=[pltpu.VMEM((tm, D_MODEL), F32), pltpu.VMEM((tm, D_MODEL), BF16)],
        compiler_params=pltpu.CompilerParams(
            dimension_semantics=("arbitrary", "arbitrary", "arbitrary"), vmem_limit_bytes=VMEM_LIMIT),
        name="ffn_moe" if gated else "ffn_dense",
    )(*args)


def _block_diag_dense(w_blk):
    per = B_DH // QK_BLOCK
    w = w_blk.reshape(B_HEADS, per, QK_BLOCK, QK_BLOCK)
    eye = jnp.eye(per, dtype=w.dtype)
    dense = eye[None, :, None, :, None] * w[:, :, :, None, :]
    return dense.reshape(B_HEADS, B_DH, B_DH)


def kernel(x_prompt, x_sample, state_hgrn, state_mlstm_c, state_mlstm_n, state_mlstm_m, state_conv, w_in, lb_param, a_norm_g, w_pa, conv_w, conv_b, w_q_blk, w_k_blk, b_ig, b_fg, b_norm_g, w_pb, w_o, ln1_g, ln1_b, ln2_g, ln2_b, w_ff_gate, w_ff_up, w_ff_down, w_router, b_router, w_ex_gate, w_ex_up, w_ex_down):
    bp, seq, _ = x_prompt.shape
    bs, dseq, _ = x_sample.shape
    depth = w_in.shape[0]
    n_prompt = bp * seq
    x = jnp.concatenate([x_prompt.reshape(n_prompt, D_MODEL), x_sample.reshape(bs * dseq, D_MODEL)], axis=0)

    gate_lo = 7 * D_MODEL
    gate_hi = gate_lo + 2 * B_HEADS
    state_in = [state_hgrn, state_mlstm_c, state_mlstm_n, state_mlstm_m.reshape(depth, bs, B_HEADS, 1), state_conv]
    st_p = None
    st_s = None
    for l in range(depth):
        wl = w_in[l]
        w_main = jnp.concatenate([wl[:, :gate_lo], wl[:, gate_hi:]], axis=1).astype(BF16)
        w_gate = jnp.pad(wl[:, gate_lo:gate_hi], ((0, 0), (0, LANES - 2 * B_HEADS))).astype(BF16)
        z, gates = _proj_in(x, w_main, w_gate)

        gbias = jnp.pad(jnp.concatenate([b_ig[l], b_fg[l]]), (0, LANES - 2 * B_HEADS)).reshape(1, LANES)
        consts = [lb_param.astype(F32), a_norm_g[l].reshape(1, D_MODEL), conv_w[l], conv_b[l].reshape(1, D_MODEL),
                  _block_diag_dense(w_q_blk[l]).astype(BF16), _block_diag_dense(w_k_blk[l]).astype(BF16),
                  gbias, b_norm_g[l].reshape(1, D_MODEL)]
        outs_p = _mixer(z, gates, 0, bp, seq, l, depth, consts, None, None, st_p)
        outs_s = _mixer(z, gates, n_prompt, bs, dseq, l, depth, consts, state_in, outs_p[:2], st_s)
        oa, hb = outs_s[:2]
        st_p = outs_p[2:]
        st_s = outs_s[2:]

        j = l // 2
        if l % 2 == 0:
            router = None
        else:
            wr = jnp.pad(w_router[j], ((0, 0), (0, LANES - N_EXPERTS)))
            br = jnp.concatenate([b_router[j].astype(F32), jnp.full((LANES - N_EXPERTS,), NEG, F32)]).reshape(1, LANES)
            router = (wr, br)
        x1, gate = _post(oa, hb, z, x, w_pa[l].astype(BF16), w_pb[l].astype(BF16), w_o[l].astype(BF16),
                         ln1_g[l].reshape(1, D_MODEL), ln1_b[l].reshape(1, D_MODEL), router)
        g2 = ln2_g[l].reshape(1, D_MODEL)
        b2 = ln2_b[l].reshape(1, D_MODEL)
        if l % 2 == 0:
            x = _ffn(x1, None, w_ff_gate[j][None].astype(BF16), w_ff_up[j][None].astype(BF16),
                     w_ff_down[j][None].astype(BF16), g2, b2)
        else:
            x = _ffn(x1, gate, w_ex_gate[j].astype(BF16), w_ex_up[j].astype(BF16), w_ex_down[j].astype(BF16), g2, b2)

    y_p = x[:n_prompt].reshape(bp, seq, D_MODEL)
    y_s = x[n_prompt:].reshape(bs, dseq, D_MODEL)
    hgrn_p, c_p, n_p, m_p, conv_p = st_p
    hgrn_s, c_s, n_s, m_s, conv_s = st_s
    m_p = m_p.reshape(depth, bp, B_HEADS)
    m_s = m_s.reshape(depth, bs, B_HEADS)
    return (y_p, y_s, hgrn_p, hgrn_s, c_p, c_s, n_p, n_s, m_p, m_s, conv_p, conv_s)
```

```python
import functools

import numpy as np
import jax
import jax.numpy as jnp
from jax import lax
from jax.experimental import pallas as pl
from jax.experimental.pallas import tpu as pltpu

F32 = jnp.float32
BF16 = jnp.bfloat16

D_MODEL = 1024
A_HEADS = 8
A_DK = 128
A_DV = 128
B_HEADS = 4
B_DH = 256
QK_BLOCK = 4
CONV_W = 4
CHUNK = 32
D_FF = 2816
N_EXPERTS = 8
DEPTH = 4
ALPHA = (2 * DEPTH) ** 0.25
EPS = 1e-5
NEG = -1e30
TINY = 1e-30
LOG2E = 1.4426950408889634
MIXER_CHUNK = 64

LANES = 128
SUBLANES = 8
VMEM_LIMIT = 56 * 1024 * 1024

COL_AQ, COL_AF, COL_AI, COL_AG, COL_BU, COL_BV, COL_BO, COL_GA, COL_GB = range(9)
N_MAIN_BLOCKS = 9


def _pick_tile(n, prefs):
    for p in prefs:
        if n % p == 0:
            return p
    return n


def _sigmoid(x):
    return 1.0 / (1.0 + jnp.exp(-x))


def _split3(x):
    hi = x.astype(BF16)
    r1 = x - hi.astype(F32)
    mid = r1.astype(BF16)
    lo = (r1 - mid.astype(F32)).astype(BF16)
    return hi, mid, lo


def _cumsum_rows(tril_bf16, x):
    hi, mid, lo = _split3(x)
    acc = jnp.dot(tril_bf16, lo, preferred_element_type=F32)
    acc = acc + jnp.dot(tril_bf16, mid, preferred_element_type=F32)
    return acc + jnp.dot(tril_bf16, hi, preferred_element_type=F32)


def _dot_nt(a, b):
    return lax.dot_general(a, b, (((1,), (1,)), ((), ())), preferred_element_type=F32)


def _dot_tn(a, b):
    return lax.dot_general(a, b, (((0,), (0,)), ((), ())), preferred_element_type=F32)


def _proj_in_kernel(x_ref, w_ref, wg_ref, z_ref, g_ref, xb_ref):
    @pl.when(pl.program_id(1) == 0)
    def _():
        xb = x_ref[...].astype(BF16)
        xb_ref[...] = xb
        g_ref[...] = jnp.dot(xb, wg_ref[...], preferred_element_type=F32)

    z_ref[...] = jnp.dot(xb_ref[...], w_ref[...], preferred_element_type=F32)


def _proj_in(x, w_main, w_gate):
    t = x.shape[0]
    tm = _pick_tile(t, (1024, 512, 256, 128, 64, 32, 16, 8))
    return pl.pallas_call(
        _proj_in_kernel,
        grid=(t // tm, N_MAIN_BLOCKS),
        in_specs=[
            pl.BlockSpec((tm, D_MODEL), lambda i, j: (i, 0)),
            pl.BlockSpec((D_MODEL, D_MODEL), lambda i, j: (0, j)),
            pl.BlockSpec((D_MODEL, LANES), lambda i, j: (0, 0)),
        ],
        out_specs=[
            pl.BlockSpec((tm, D_MODEL), lambda i, j: (i, j)),
            pl.BlockSpec((tm, LANES), lambda i, j: (i, 0)),
        ],
        out_shape=[
            jax.ShapeDtypeStruct((t, N_MAIN_BLOCKS * D_MODEL), F32),
            jax.ShapeDtypeStruct((t, LANES), F32),
        ],
        scratch_shapes=[pltpu.VMEM((tm, D_MODEL), BF16)],
        compiler_params=pltpu.CompilerParams(
            dimension_semantics=("arbitrary", "arbitrary"), vmem_limit_bytes=VMEM_LIMIT),
        name="proj_in",
    )(x, w_main, w_gate)


def _level_tables(c):
    levels = c.bit_length() - 1
    assert 1 << levels == c
    blocks = []
    for l in range(levels):
        a = np.zeros((c, c), np.float32)
        for r in range(c):
            right_start = ((r >> (l + 1)) << (l + 1)) + (1 << l)
            if (r >> l) & 1:
                a[r, right_start:r + 1] = 1.0
            else:
                a[r, r + 1:right_start] = 1.0
        blocks.append(a)
    blocks.append(np.tril(np.ones((c, c), np.float32)))
    blocks.append(np.triu(np.ones((c, c), np.float32), k=1))
    lv = np.full((c, c), -1, np.int32)
    for t in range(c):
        for s_ in range(t):
            lv[t, s_] = (t ^ s_).bit_length() - 1
        lv[t, t] = levels
    return np.concatenate(blocks, axis=0), lv, levels


def _mixer_kernel(*refs, layer, lt, c, levels, has_state, n_alias, nsteps):
    (aq_ref, af_ref, ai_ref, ag_ref, bu_ref, bv_ref, bo_ref, gt_ref,
     lbp_ref, ang_ref, cw_ref, cb_ref, wq_ref, wk_ref, gbias_ref, bng_ref,
     atab_ref, tril_ref, lv_ref) = refs[:19]
    pos = 19
    if has_state:
        s0_ref, c0_ref, n0_ref, m0_ref, cv0_ref = refs[pos:pos + 5]
        pos += 5
    pos += n_alias
    oa_ref, hb_ref, so_ref, co_ref, no_ref, mo_ref, cvo_ref = refs[pos:pos + 7]
    pos += 7
    st_ref, c_ref, n_ref, m_ref, xp_ref, qb_ref, kb_ref = refs[pos:]

    step = pl.program_id(1)
    halo = SUBLANES

    @pl.when(step == 0)
    def _():
        xp_ref[0:halo, :] = jnp.zeros((halo, D_MODEL), F32)
        if has_state:
            for h in range(A_HEADS):
                st_ref[h] = s0_ref[0, h].T
            c_ref[...] = c0_ref[0]
            n_ref[0:B_HEADS, :] = n0_ref[0]
            m0 = m0_ref[0]
            for h in range(B_HEADS):
                m_ref[h] = jnp.broadcast_to(m0[h:h + 1, :], (SUBLANES, LANES))
            xp_ref[halo - (CONV_W - 1):halo, :] = cv0_ref[0]
        else:
            st_ref[...] = jnp.zeros_like(st_ref)
            c_ref[...] = jnp.zeros_like(c_ref)
            n_ref[...] = jnp.zeros_like(n_ref)
            m_ref[...] = jnp.zeros_like(m_ref)

    xp_ref[halo:halo + lt, :] = bu_ref[...]
    y = cb_ref[...]
    for j in range(CONV_W):
        y = y + xp_ref[halo - (CONV_W - 1) + j:halo - (CONV_W - 1) + j + lt, :] * cw_ref[j:j + 1, :]
    tail = xp_ref[halo + lt - (CONV_W - 1):halo + lt, :]
    xp_ref[halo - (CONV_W - 1):halo, :] = tail
    u = (y * _sigmoid(y)).astype(BF16)
    for g in range(B_HEADS):
        sl = slice(g * B_DH, (g + 1) * B_DH)
        qb_ref[:, sl] = jnp.dot(u[:, sl], wq_ref[g], preferred_element_type=F32)
        kb_ref[:, sl] = jnp.dot(u[:, sl], wk_ref[g], preferred_element_type=F32) * (B_DH ** -0.5)

    lbp = lbp_ref[...]
    lbe = jnp.exp(lbp - jnp.max(lbp, axis=0, keepdims=True))
    lbs = lbe / jnp.sum(lbe, axis=0, keepdims=True)
    lb = jnp.zeros((1, D_MODEL), F32)
    for r in range(1, layer + 1):
        lb = lb + lbs[r:r + 1, :]
    one_m_lb = 1.0 - lb

    lv = lv_ref[...]
    tril = tril_ref[...]
    causal = lv >= 0
    masks = [lv == l for l in range(levels + 1)]
    lane = lax.broadcasted_iota(jnp.int32, (c, LANES), 1)

    def chunk(r0):
        rs = pl.ds(r0, c)
        fp = af_ref[rs, :]
        e = jnp.exp(-jnp.abs(fp))
        r = 1.0 / (1.0 + e)
        er = e * r
        sig_p = jnp.where(fp >= 0, r, er)
        sig_n = jnp.where(fp >= 0, er, r)
        f_gate = lb + one_m_lb * sig_p
        lf2 = jnp.log(jnp.maximum(f_gate, TINY)) * LOG2E
        key = one_m_lb * sig_n
        aq = aq_ref[rs, :]
        q = aq * _sigmoid(aq)
        vb = ai_ref[rs, :].astype(BF16)
        hi, mid, lo = _split3(lf2)
        if c >= 2 * SUBLANES:
            x3 = jnp.concatenate([lo, mid, hi], axis=0)
            eg = jnp.dot(atab_ref[...], x3, preferred_element_type=F32)
        else:
            atab = atab_ref[...]
            eg = jnp.dot(atab, lo, preferred_element_type=F32)
            eg = eg + jnp.dot(atab, mid, preferred_element_type=F32)
            eg = eg + jnp.dot(atab, hi, preferred_element_type=F32)

        gt = gt_ref[rs, :] + gbias_ref[...]
        lf = jnp.minimum(gt, 0.0) - jnp.log(1.0 + jnp.exp(-jnp.abs(gt)))
        bcum = _cumsum_rows(tril, lf)
        qm = qb_ref[rs, :]
        km = kb_ref[rs, :]
        qmb = qm.astype(BF16)
        kmb = km.astype(BF16)
        vmb = bv_ref[rs, :].astype(BF16)

        b_col, i_col, m_prev, d_intra, qk, qc, cm, nv = [], [], [], [], [], [], [], []
        for h in range(B_HEADS):
            hs = slice(h * B_DH, (h + 1) * B_DH)
            b_col.append(bcum[:, B_HEADS + h:B_HEADS + h + 1])
            i_col.append(gt[:, h:h + 1])
            m_prev.append(m_ref[h][0:1, 0:1])
            lhs = jnp.where(lane == 0, b_col[h], jnp.where(lane == 1, 1.0, 0.0))
            rhs = jnp.where(lane == 0, 1.0, jnp.where(lane == 1, i_col[h] - b_col[h], 0.0))
            d_intra.append(lax.dot_general(lhs, rhs, (((1,), (1,)), ((), ())),
                                           precision=lax.Precision.HIGHEST, preferred_element_type=F32))
            qk.append(_dot_nt(qmb[:, hs], kmb[:, hs]))
            cm.append(c_ref[h])
            nv.append(n_ref[h:h + 1, :])
            qc.append(jnp.dot(qmb[:, hs], cm[h].astype(BF16), preferred_element_type=F32))

        w_all = jnp.exp2(eg)
        wg = w_all[levels * c:(levels + 1) * c, :]
        wk = w_all[(levels + 1) * c:(levels + 2) * c, :]
        qe = (q * wg).astype(BF16)
        kd = (key * wk).astype(BF16)
        lvl = []
        for l in range(levels):
            wl = w_all[l * c:(l + 1) * c, :]
            qw = (q * wl).astype(BF16)
            kw = (key * wl).astype(BF16)
            lvl.append([_dot_nt(qw[:, h * A_DK:(h + 1) * A_DK], kw[:, h * A_DK:(h + 1) * A_DK])
                        for h in range(A_HEADS)])
        qb16 = q.astype(BF16)
        kb16 = key.astype(BF16)
        lvl.append([_dot_nt(qb16[:, h * A_DK:(h + 1) * A_DK], kb16[:, h * A_DK:(h + 1) * A_DK])
                    for h in range(A_HEADS)])
        st = [st_ref[h] for h in range(A_HEADS)]
        inter = [_dot_nt(qe[:, h * A_DK:(h + 1) * A_DK], st[h].astype(BF16)) for h in range(A_HEADS)]

        att = []
        for h in range(A_HEADS):
            a = jnp.where(masks[levels], lvl[levels][h], 0.0)
            for l in range(levels):
                a = jnp.where(masks[l], lvl[l][h], a)
            att.append(a.astype(BF16))
        smat, w_inter, m_t = [], [], []
        for h in range(B_HEADS):
            d = jnp.where(causal, d_intra[h], NEG)
            d_inter = b_col[h] + m_prev[h]
            m_t.append(jnp.maximum(jnp.max(d, axis=-1, keepdims=True), d_inter))
            smat.append(qk[h] * jnp.exp(d - m_t[h]))
            w_inter.append(jnp.exp(d_inter - m_t[h]))

        o_intra = [jnp.dot(att[h], vb[:, h * A_DV:(h + 1) * A_DV], preferred_element_type=F32)
                   for h in range(A_HEADS)]
        s_upd = [_dot_tn(vb[:, h * A_DV:(h + 1) * A_DV], kd[:, h * A_DK:(h + 1) * A_DK]) for h in range(A_HEADS)]
        num_a, c_upd, kws, m_new, decay = [], [], [], [], []
        for h in range(B_HEADS):
            hs = slice(h * B_DH, (h + 1) * B_DH)
            num_a.append(jnp.dot(smat[h].astype(BF16), vmb[:, hs], preferred_element_type=F32))
            m_new.append(m_t[h][c - 1:c, :])
            b_last = b_col[h][c - 1:c, :]
            w_state = jnp.exp(b_last - b_col[h] + i_col[h] - m_new[h])
            decay.append(jnp.exp(b_last + m_prev[h] - m_new[h]))
            kws.append(km[:, hs] * w_state)
            c_upd.append(_dot_tn(kws[h].astype(BF16), vmb[:, hs]))

        ag = ag_ref[rs, :]
        out_gate = ag * _sigmoid(ag)
        ang = ang_ref[...]
        for h in range(A_HEADS):
            hs = slice(h * A_DK, (h + 1) * A_DK)
            o_h = o_intra[h] + inter[h]
            st_ref[h] = st[h] * wg[c - 1:c, hs] + s_upd[h]
            ms = jnp.mean(o_h * o_h, axis=-1, keepdims=True)
            o_h = o_h * lax.rsqrt(ms + EPS) * ang[:, hs] * out_gate[:, hs]
            oa_ref[rs, hs] = o_h.astype(oa_ref.dtype)
        bo = bo_ref[rs, :]
        bng = bng_ref[...]
        for h in range(B_HEADS):
            hs = slice(h * B_DH, (h + 1) * B_DH)
            num = num_a[h] + w_inter[h] * qc[h]
            den = jnp.sum(smat[h], axis=-1, keepdims=True) + w_inter[h] * jnp.sum(
                qm[:, hs] * nv[h], axis=-1, keepdims=True)
            hh = num / jnp.maximum(jnp.abs(den), jnp.exp(-m_t[h]))
            c_ref[h] = decay[h] * cm[h] + c_upd[h]
            n_ref[h:h + 1, :] = decay[h] * nv[h] + jnp.sum(kws[h], axis=0, keepdims=True)
            m_ref[h] = jnp.broadcast_to(m_new[h], (SUBLANES, LANES))
            mu = jnp.mean(hh, axis=-1, keepdims=True)
            xc = hh - mu
            var = jnp.mean(xc * xc, axis=-1, keepdims=True)
            hn = xc * lax.rsqrt(var + EPS) * bng[:, hs]
            hb_ref[rs, hs] = (_sigmoid(bo[:, hs]) * hn).astype(hb_ref.dtype)

    nch = lt // c
    if nch == 1:
        chunk(0)
    else:
        def body(ci, carry):
            chunk(pl.multiple_of(ci * c, c))
            return carry
        lax.fori_loop(0, nch, body, 0)

    @pl.when(step == nsteps - 1)
    def _():
        for h in range(A_HEADS):
            so_ref[0, h] = st_ref[h].T
        co_ref[0] = c_ref[...]
        no_ref[0] = n_ref[0:B_HEADS, :]
        for h in range(B_HEADS):
            mo_ref[0, h:h + 1, :] = m_ref[h][0:1, 0:1]
        cvo_ref[0] = xp_ref[halo - (CONV_W - 1):halo, :]


def _state_shapes(depth, nseq):
    return [(depth, nseq, A_HEADS, A_DK, A_DV), (depth, nseq, B_HEADS, B_DH, B_DH), (depth, nseq, B_HEADS, B_DH),
            (depth, nseq, B_HEADS, 1), (depth, nseq, CONV_W - 1, D_MODEL)]


def _mixer(z, gates, row0, nseq, seqlen, layer, depth, consts, state, prev_tok, prev_state):
    c = min(seqlen, MIXER_CHUNK) if seqlen % CHUNK == 0 else seqlen
    lt = _pick_tile(seqlen, (256, 128, 64, 32)) if seqlen % CHUNK == 0 else seqlen
    assert lt % c == 0
    nsteps = seqlen // lt
    blk0 = row0 // lt
    has_state = state is not None
    a_all, lv, levels = _level_tables(c)
    if c >= 2 * SUBLANES:
        a_all = np.concatenate([a_all] * 3, axis=1)
    tables = [jnp.asarray(a_all, BF16), jnp.asarray(np.tril(np.ones((c, c), np.float32)), BF16), jnp.asarray(lv)]

    def zspec(colblk):
        return pl.BlockSpec((lt, D_MODEL), lambda b, i, cb=colblk: (blk0 + b * nsteps + i, cb))

    def full(shape):
        nd = len(shape)
        return pl.BlockSpec(shape, lambda b, i: (0,) * nd)

    def per_seq(shape):
        nd = len(shape) - 2
        return pl.BlockSpec((None, 1) + tuple(shape[2:]), lambda b, i: (layer, b) + (0,) * nd)

    in_specs = [zspec(COL_AQ), zspec(COL_AF), zspec(COL_AI), zspec(COL_AG), zspec(COL_BU), zspec(COL_BV),
                zspec(COL_BO),
                pl.BlockSpec((lt, LANES), lambda b, i: (blk0 + b * nsteps + i, 0))]
    args = [z] * 7 + [gates]
    for a in list(consts) + tables:
        in_specs.append(full(a.shape))
        args.append(a)
    if has_state:
        for a in state:
            in_specs.append(per_seq(a.shape))
            args.append(a)
    aliases = {}
    n_alias = 0
    for first_out, prev in ((0, prev_tok), (2, prev_state)):
        if prev is not None:
            for k, a in enumerate(prev):
                aliases[len(args)] = first_out + k
                in_specs.append(pl.BlockSpec(memory_space=pl.ANY))
                args.append(a)
            n_alias += len(prev)

    tok = z.shape[0]
    st_shapes = _state_shapes(depth, nseq)
    out_shape = [jax.ShapeDtypeStruct((tok, D_MODEL), F32), jax.ShapeDtypeStruct((tok, D_MODEL), F32)]
    out_shape += [jax.ShapeDtypeStruct(s, F32) for s in st_shapes]
    tok_spec = pl.BlockSpec((lt, D_MODEL), lambda b, i: (blk0 + b * nsteps + i, 0))
    out_specs = [tok_spec, tok_spec] + [per_seq(s) for s in st_shapes]
    scratch = [
        pltpu.VMEM((A_HEADS, A_DV, A_DK), F32),
        pltpu.VMEM((B_HEADS, B_DH, B_DH), F32),
        pltpu.VMEM((SUBLANES, B_DH), F32),
        pltpu.VMEM((B_HEADS, SUBLANES, LANES), F32),
        pltpu.VMEM((lt + SUBLANES, D_MODEL), F32),
        pltpu.VMEM((lt, D_MODEL), F32),
        pltpu.VMEM((lt, D_MODEL), F32),
    ]
    kern = functools.partial(_mixer_kernel, layer=layer, lt=lt, c=c, levels=levels, has_state=has_state,
                             n_alias=n_alias, nsteps=nsteps)
    return pl.pallas_call(
        kern,
        grid=(nseq, nsteps),
        in_specs=in_specs,
        out_specs=out_specs,
        out_shape=out_shape,
        scratch_shapes=scratch,
        input_output_aliases=aliases,
        compiler_params=pltpu.CompilerParams(
            dimension_semantics=("arbitrary", "arbitrary"), vmem_limit_bytes=VMEM_LIMIT),
        name="mixer_state" if has_state else "mixer_prompt",
    )(*args)


def _layer_norm_rows(r, g, b):
    mu = jnp.mean(r, axis=-1, keepdims=True)
    xc = r - mu
    var = jnp.mean(xc * xc, axis=-1, keepdims=True)
    return xc * lax.rsqrt(var + EPS) * g + b


def _post_kernel(*refs, with_router):
    oa_ref, hb_ref, ga_ref, gb_ref, x_ref, wpa_ref, wpb_ref, wo_ref, g1_ref, b1_ref = refs[:10]
    if with_router:
        wr_ref, br_ref, out_ref, xb_ref, info_ref, carry_ref, cnt_ref = refs[10:]
    else:
        (out_ref,) = refs[10:]
    ya = jnp.dot(oa_ref[...].astype(BF16), wpa_ref[...], preferred_element_type=F32)
    yb = jnp.dot(hb_ref[...].astype(BF16), wpb_ref[...], preferred_element_type=F32)
    merged = _sigmoid(ga_ref[...]) * ya + _sigmoid(gb_ref[...]) * yb
    o = jnp.dot(merged.astype(BF16), wo_ref[...], preferred_element_type=F32)
    x1 = _layer_norm_rows(ALPHA * x_ref[...] + o, g1_ref[...], b1_ref[...])
    out_ref[...] = x1
    if with_router:
        logits = jnp.dot(x1, wr_ref[...], precision=lax.Precision.HIGHEST,
                         preferred_element_type=F32) + br_ref[...]
        lane = lax.broadcasted_iota(jnp.int32, logits.shape, 1)
        m1 = jnp.max(logits, axis=-1, keepdims=True)
        i1 = jnp.min(jnp.where(logits == m1, lane, LANES), axis=-1, keepdims=True)
        rest = jnp.where(lane == i1, NEG * 2.0, logits)
        m2 = jnp.max(rest, axis=-1, keepdims=True)
        i2 = jnp.min(jnp.where(rest == m2, lane, LANES), axis=-1, keepdims=True)
        e2 = jnp.exp(m2 - m1)
        w1 = 1.0 / (1.0 + e2)
        w2 = e2 * w1
        xb_ref[...] = x1.astype(BF16)

        @pl.when(pl.program_id(0) == 0)
        def _():
            cnt_ref[...] = jnp.zeros_like(cnt_ref)

        tm = logits.shape[0]
        sel1 = lane == i1
        sel2 = lane == i2
        onehot = jnp.where(sel1, 1.0, jnp.where(sel2, 1.0, 0.0))
        strict = (lax.broadcasted_iota(jnp.int32, (tm, tm), 0) > lax.broadcasted_iota(jnp.int32, (tm, tm), 1))
        before = jnp.dot(jnp.where(strict, 1.0, 0.0).astype(BF16), onehot.astype(BF16),
                         preferred_element_type=F32) + cnt_ref[0:1, :]
        rank1 = jnp.sum(jnp.where(sel1, before, 0.0), axis=-1, keepdims=True)
        rank2 = jnp.sum(jnp.where(sel2, before, 0.0), axis=-1, keepdims=True)
        total = jnp.broadcast_to(cnt_ref[0:1, :] + jnp.sum(onehot, axis=0, keepdims=True), (SUBLANES, LANES))
        cnt_ref[...] = total
        carry_ref[0] = total
        info = jnp.where(lane == 0, i1.astype(F32), jnp.where(lane == 1, i2.astype(F32), jnp.where(
            lane == 2, w1, jnp.where(lane == 3, w2, jnp.where(lane == 4, rank1, jnp.where(lane == 5, rank2, 0.0))))))
        info_ref[...] = info


def _post(oa, hb, z, x, wpa, wpb, wo, g1, b1, router):
    t = x.shape[0]
    tm = _pick_tile(t, (512, 256, 128, 64, 32, 16, 8))
    tok = pl.BlockSpec((tm, D_MODEL), lambda i: (i, 0))
    wspec = pl.BlockSpec((D_MODEL, D_MODEL), lambda i: (0, 0))
    vec = pl.BlockSpec((1, D_MODEL), lambda i: (0, 0))
    in_specs = [tok, tok,
                pl.BlockSpec((tm, D_MODEL), lambda i: (i, COL_GA)),
                pl.BlockSpec((tm, D_MODEL), lambda i: (i, COL_GB)),
                tok, wspec, wspec, wspec, vec, vec]
    args = [oa, hb, z, z, x, wpa, wpb, wo, g1, b1]
    out_shape = [jax.ShapeDtypeStruct((t, D_MODEL), F32)]
    out_specs = [tok]
    scratch = []
    if router is not None:
        in_specs += [pl.BlockSpec((D_MODEL, LANES), lambda i: (0, 0)), pl.BlockSpec((1, LANES), lambda i: (0, 0))]
        args += list(router)
        out_shape += [jax.ShapeDtypeStruct((t, D_MODEL), BF16), jax.ShapeDtypeStruct((t, LANES), F32),
                      jax.ShapeDtypeStruct((t // tm, SUBLANES, LANES), F32)]
        out_specs += [tok, pl.BlockSpec((tm, LANES), lambda i: (i, 0)),
                      pl.BlockSpec((1, SUBLANES, LANES), lambda i: (i, 0, 0))]
        scratch = [pltpu.VMEM((SUBLANES, LANES), F32)]
    res = pl.pallas_call(
        functools.partial(_post_kernel, with_router=router is not None),
        grid=(t // tm,),
        in_specs=in_specs,
        out_specs=out_specs,
        out_shape=out_shape,
        scratch_shapes=scratch,
        compiler_params=pltpu.CompilerParams(dimension_semantics=("arbitrary",), vmem_limit_bytes=VMEM_LIMIT),
        name="post_router" if router is not None else "post",
    )(*args)
    return res, tm


def _ffn_kernel(x_ref, wg_ref, wu_ref, wd_ref, g2_ref, b2_ref, out_ref, acc_ref, xb_ref, *, n_f):
    j = pl.program_id(1)

    @pl.when(j == 0)
    def _():
        acc_ref[...] = jnp.zeros_like(acc_ref)
        xb_ref[...] = x_ref[...].astype(BF16)

    xb = xb_ref[...]
    hg = jnp.dot(xb, wg_ref[...], preferred_element_type=F32)
    hu = jnp.dot(xb, wu_ref[...], preferred_element_type=F32)
    hact = (hg * _sigmoid(hg) * hu).astype(BF16)
    acc_ref[...] += jnp.dot(hact, wd_ref[...], preferred_element_type=F32)

    @pl.when(j == n_f - 1)
    def _():
        out_ref[...] = _layer_norm_rows(ALPHA * x_ref[...] + acc_ref[...], g2_ref[...], b2_ref[...])


def _ffn(x, wg, wu, wd, g2, b2):
    t = x.shape[0]
    tm = _pick_tile(t, (512, 256, 128, 64, 32, 16, 8))
    tf = D_FF // 2
    n_f = D_FF // tf
    tok = pl.BlockSpec((tm, D_MODEL), lambda i, j: (i, 0))
    vec = pl.BlockSpec((1, D_MODEL), lambda i, j: (0, 0))
    return pl.pallas_call(
        functools.partial(_ffn_kernel, n_f=n_f),
        grid=(t // tm, n_f),
        in_specs=[tok,
                  pl.BlockSpec((D_MODEL, tf), lambda i, j: (0, j)),
                  pl.BlockSpec((D_MODEL, tf), lambda i, j: (0, j)),
                  pl.BlockSpec((tf, D_MODEL), lambda i, j: (j, 0)),
                  vec, vec],
        out_specs=tok,
        out_shape=jax.ShapeDtypeStruct((t, D_MODEL), F32),
        scratch_shapes=[pltpu.VMEM((tm, D_MODEL), F32), pltpu.VMEM((tm, D_MODEL), BF16)],
        compiler_params=pltpu.CompilerParams(
            dimension_semantics=("arbitrary", "arbitrary"), vmem_limit_bytes=VMEM_LIMIT),
        name="ffn_dense",
    )(x, wg, wu, wd, g2, b2)


MOE_ROW_TILE = 256

_FIRST, _LAST, _VALID = 1, 2, 4


def _route_items(counts_lo, counts_hi, row_tile, n_items, major_is_expert):
    n_e, n_w = counts_lo.shape
    n = jnp.where(counts_hi > counts_lo, (counts_hi - 1) // row_tile - counts_lo // row_tile + 1, 0)
    first = counts_lo // row_tile
    e_idx = jnp.broadcast_to(jnp.arange(n_e, dtype=jnp.int32)[:, None], (n_e, n_w))
    w_idx = jnp.broadcast_to(jnp.arange(n_w, dtype=jnp.int32)[None, :], (n_e, n_w))
    if not major_is_expert:
        n, first, e_idx, w_idx = n.T, first.T, e_idx.T, w_idx.T
    n, first, e_idx, w_idx = n.reshape(-1), first.reshape(-1), e_idx.reshape(-1), w_idx.reshape(-1)
    ends = jnp.cumsum(n)
    starts = ends - n
    total = ends[-1]
    k = jnp.minimum(jnp.arange(n_items, dtype=jnp.int32), total - 1)
    g = jnp.minimum(jnp.searchsorted(ends, k, side="right"), n.shape[0] - 1).astype(jnp.int32)
    tile = (first[g] + (k - starts[g])).astype(jnp.int32)
    win = w_idx[g]
    exp = e_idx[g]
    major = tile if major_is_expert else win
    valid = jnp.arange(n_items) < total
    prev_differs = jnp.concatenate([jnp.ones((1,), bool), major[1:] != major[:-1]])
    next_differs = jnp.concatenate([major[1:] != major[:-1], jnp.ones((1,), bool)])
    is_last = next_differs | (jnp.arange(n_items) == total - 1)
    flags = jnp.where(valid, _VALID + jnp.where(prev_differs, _FIRST, 0) + jnp.where(is_last, _LAST, 0), 0)
    return tile, win, exp, flags.astype(jnp.int32)


def _moe_ffn_kernel(tile_ref, win_ref, exp_ref, flag_ref, xw_ref, info_ref, wg_ref, wu_ref, wd_ref, ys_in_ref,
                    ys_ref, xs_ref, gr_ref):
    del win_ref, exp_ref, ys_in_ref
    k = pl.program_id(0)
    flags = flag_ref[k]
    tr = xs_ref.shape[0]

    @pl.when((flags & _VALID) != 0)
    def _():
        rows = (tile_ref[k] * tr + lax.broadcasted_iota(jnp.int32, (tr, 1), 0)).astype(F32)
        info = info_ref[...]
        p1 = info[0:1, :] == rows
        p2 = info[1:2, :] == rows
        sel = jnp.where(p1, 1.0, jnp.where(p2, 1.0, 0.0)).astype(BF16)
        xs = jnp.dot(sel, xw_ref[...], preferred_element_type=F32)
        gr = jnp.sum(jnp.where(p1, info[2:3, :], jnp.where(p2, info[3:4, :], 0.0)), axis=-1, keepdims=True)

        @pl.when((flags & _FIRST) != 0)
        def _():
            xs_ref[...] = xs
            gr_ref[...] = gr

        @pl.when((flags & _FIRST) == 0)
        def _():
            xs_ref[...] += xs
            gr_ref[...] += gr

        @pl.when((flags & _LAST) != 0)
        def _():
            xb = xs_ref[...].astype(BF16)
            hg = jnp.dot(xb, wg_ref[...], preferred_element_type=F32)
            hu = jnp.dot(xb, wu_ref[...], preferred_element_type=F32)
            hact = (hg * _sigmoid(hg) * hu).astype(BF16)
            y = jnp.dot(hact, wd_ref[...], preferred_element_type=F32)
            ys_ref[...] = (y * gr_ref[...]).astype(ys_ref.dtype)


def _moe_combine_kernel(win_ref, blk_ref, flag_ref, cinfo_ref, ys_ref, x_ref, g2_ref, b2_ref, out_ref, acc_ref):
    del win_ref
    k = pl.program_id(0)
    flags = flag_ref[k]
    tr = ys_ref.shape[0]

    @pl.when((flags & _VALID) != 0)
    def _():
        rows = (blk_ref[k] * tr + lax.broadcasted_iota(jnp.int32, (1, tr), 1)).astype(F32)
        cinfo = cinfo_ref[...]
        sel = jnp.where(cinfo[:, 0:1] == rows, 1.0, jnp.where(cinfo[:, 1:2] == rows, 1.0, 0.0)).astype(BF16)
        part = jnp.dot(sel, ys_ref[...], preferred_element_type=F32)

        @pl.when((flags & _FIRST) != 0)
        def _():
            acc_ref[...] = part

        @pl.when((flags & _FIRST) == 0)
        def _():
            acc_ref[...] += part

        @pl.when((flags & _LAST) != 0)
        def _():
            out_ref[...] = _layer_norm_rows(ALPHA * x_ref[...] + acc_ref[...], g2_ref[...], b2_ref[...])


def _moe(x1, x1b, info, carry, tm_router, wg, wu, wd, g2, b2):
    t = x1.shape[0]
    n_e = wg.shape[0]
    tr = MOE_ROW_TILE
    wtok = 2 * tm_router if t % (2 * tm_router) == 0 else tm_router
    n_w = t // wtok
    n_rt = -(-2 * t // tr) + n_e
    n_rows = n_rt * tr
    n_items = n_rt + n_e * (n_w - 1)

    e1 = info[:, 0].astype(jnp.int32)
    e2 = info[:, 1].astype(jnp.int32)
    cum_after = carry[:, 0, :n_e].astype(jnp.int32)
    counts = cum_after[-1]
    padded = -(-counts // tr) * tr
    off = jnp.cumsum(padded) - padded
    pos1 = (off[e1] + info[:, 4].astype(jnp.int32)).astype(F32)
    pos2 = (off[e2] + info[:, 5].astype(jnp.int32)).astype(F32)
    per_w = wtok // tm_router
    cum_w = jnp.concatenate([jnp.zeros((1, n_e), jnp.int32), cum_after[per_w - 1::per_w]], axis=0)
    lo = (off[None, :] + cum_w[:-1]).T
    hi = (off[None, :] + cum_w[1:]).T
    row_info = jnp.stack([pos1, pos2, info[:, 2], info[:, 3]] + [jnp.zeros((t,), F32)] * 4, axis=0)
    col_info = row_info.T

    tile_a, win_a, exp_a, flag_a = _route_items(lo, hi, tr, n_items, True)
    blk_b, win_b, _, flag_b = _route_items(lo, hi, tr, n_items, False)

    wspec_in = pl.BlockSpec((None, D_MODEL, D_FF), lambda k, ti, wi, ex, fl: (ex[k], 0, 0),
                            pipeline_mode=pl.Buffered(1))
    wspec_out = pl.BlockSpec((None, D_FF, D_MODEL), lambda k, ti, wi, ex, fl: (ex[k], 0, 0),
                             pipeline_mode=pl.Buffered(1))
    ys = pl.pallas_call(
        _moe_ffn_kernel,
        grid_spec=pltpu.PrefetchScalarGridSpec(
            num_scalar_prefetch=4,
            grid=(n_items,),
            in_specs=[
                pl.BlockSpec((wtok, D_MODEL), lambda k, ti, wi, ex, fl: (wi[k], 0)),
                pl.BlockSpec((SUBLANES, wtok), lambda k, ti, wi, ex, fl: (0, wi[k])),
                wspec_in, wspec_in, wspec_out,
                pl.BlockSpec(memory_space=pl.ANY),
            ],
            out_specs=pl.BlockSpec((tr, D_MODEL), lambda k, ti, wi, ex, fl: (ti[k], 0)),
            scratch_shapes=[pltpu.VMEM((tr, D_MODEL), F32), pltpu.VMEM((tr, 1), F32)],
        ),
        out_shape=jax.ShapeDtypeStruct((n_rows, D_MODEL), BF16),
        input_output_aliases={9: 0},
        compiler_params=pltpu.CompilerParams(dimension_semantics=("arbitrary",), vmem_limit_bytes=VMEM_LIMIT),
        name="moe_ffn",
    )(tile_a, win_a, exp_a, flag_a, x1b, row_info, wg, wu, wd, jnp.zeros((n_rows, D_MODEL), BF16))

    return pl.pallas_call(
        _moe_combine_kernel,
        grid_spec=pltpu.PrefetchScalarGridSpec(
            num_scalar_prefetch=3,
            grid=(n_items,),
            in_specs=[
                pl.BlockSpec((wtok, SUBLANES), lambda k, wi, bi, fl: (wi[k], 0)),
                pl.BlockSpec((tr, D_MODEL), lambda k, wi, bi, fl: (bi[k], 0)),
                pl.BlockSpec((wtok, D_MODEL), lambda k, wi, bi, fl: (wi[k], 0)),
                pl.BlockSpec((1, D_MODEL), lambda k, wi, bi, fl: (0, 0)),
                pl.BlockSpec((1, D_MODEL), lambda k, wi, bi, fl: (0, 0)),
            ],
            out_specs=pl.BlockSpec((wtok, D_MODEL), lambda k, wi, bi, fl: (wi[k], 0)),
            scratch_shapes=[pltpu.VMEM((wtok, D_MODEL), F32)],
        ),
        out_shape=jax.ShapeDtypeStruct((t, D_MODEL), F32),
        compiler_params=pltpu.CompilerParams(dimension_semantics=("arbitrary",), vmem_limit_bytes=VMEM_LIMIT),
        name="moe_combine",
    )(win_b, blk_b, flag_b, col_info, ys, x1, g2, b2)


def _block_diag_dense(w_blk):
    per = B_DH // QK_BLOCK
    w = w_blk.reshape(B_HEADS, per, QK_BLOCK, QK_BLOCK)
    eye = jnp.eye(per, dtype=w.dtype)
    dense = eye[None, :, None, :, None] * w[:, :, :, None, :]
    return dense.reshape(B_HEADS, B_DH, B_DH)


def kernel(x_prompt, x_sample, state_hgrn, state_mlstm_c, state_mlstm_n, state_mlstm_m, state_conv, w_in, lb_param, a_norm_g, w_pa, conv_w, conv_b, w_q_blk, w_k_blk, b_ig, b_fg, b_norm_g, w_pb, w_o, ln1_g, ln1_b, ln2_g, ln2_b, w_ff_gate, w_ff_up, w_ff_down, w_router, b_router, w_ex_gate, w_ex_up, w_ex_down):
    bp, seq, _ = x_prompt.shape
    bs, dseq, _ = x_sample.shape
    depth = w_in.shape[0]
    n_prompt = bp * seq
    x = jnp.concatenate([x_prompt.reshape(n_prompt, D_MODEL), x_sample.reshape(bs * dseq, D_MODEL)], axis=0)

    gate_lo = 7 * D_MODEL
    gate_hi = gate_lo + 2 * B_HEADS
    state_in = [state_hgrn, state_mlstm_c, state_mlstm_n, state_mlstm_m.reshape(depth, bs, B_HEADS, 1), state_conv]
    st_p = None
    st_s = None
    for l in range(depth):
        wl = w_in[l]
        w_main = jnp.concatenate([wl[:, :gate_lo], wl[:, gate_hi:]], axis=1).astype(BF16)
        w_gate = jnp.pad(wl[:, gate_lo:gate_hi], ((0, 0), (0, LANES - 2 * B_HEADS))).astype(BF16)
        z, gates = _proj_in(x, w_main, w_gate)

        gbias = jnp.pad(jnp.concatenate([b_ig[l], b_fg[l]]), (0, LANES - 2 * B_HEADS)).reshape(1, LANES)
        consts = [lb_param.astype(F32), a_norm_g[l].reshape(1, D_MODEL), conv_w[l], conv_b[l].reshape(1, D_MODEL),
                  _block_diag_dense(w_q_blk[l]).astype(BF16), _block_diag_dense(w_k_blk[l]).astype(BF16),
                  gbias, b_norm_g[l].reshape(1, D_MODEL)]
        outs_p = _mixer(z, gates, 0, bp, seq, l, depth, consts, None, None, st_p)
        outs_s = _mixer(z, gates, n_prompt, bs, dseq, l, depth, consts, state_in, outs_p[:2], st_s)
        oa, hb = outs_s[:2]
        st_p = outs_p[2:]
        st_s = outs_s[2:]

        j = l // 2
        if l % 2 == 0:
            router = None
        else:
            wr = jnp.pad(w_router[j], ((0, 0), (0, LANES - N_EXPERTS)))
            br = jnp.concatenate([b_router[j].astype(F32), jnp.full((LANES - N_EXPERTS,), NEG, F32)]).reshape(1, LANES)
            router = (wr, br)
        res, tm_router = _post(oa, hb, z, x, w_pa[l].astype(BF16), w_pb[l].astype(BF16), w_o[l].astype(BF16),
                               ln1_g[l].reshape(1, D_MODEL), ln1_b[l].reshape(1, D_MODEL), router)
        g2 = ln2_g[l].reshape(1, D_MODEL)
        b2 = ln2_b[l].reshape(1, D_MODEL)
        if l % 2 == 0:
            x = _ffn(res[0], w_ff_gate[j].astype(BF16), w_ff_up[j].astype(BF16), w_ff_down[j].astype(BF16), g2, b2)
        else:
            x1, x1b, info, carry = res
            x = _moe(x1, x1b, info, carry, tm_router, w_ex_gate[j].astype(BF16), w_ex_up[j].astype(BF16),
                     w_ex_down[j].astype(BF16), g2, b2)

    y_p = x[:n_prompt].reshape(bp, seq, D_MODEL)
    y_s = x[n_prompt:].reshape(bs, dseq, D_MODEL)
    hgrn_p, c_p, n_p, m_p, conv_p = st_p
    hgrn_s, c_s, n_s, m_s, conv_s = st_s
    m_p = m_p.reshape(depth, bp, B_HEADS)
    m_s = m_s.reshape(depth, bs, B_HEADS)
    return (y_p, y_s, hgrn_p, hgrn_s, c_p, c_s, n_p, n_s, m_p, m_s, conv_p, conv_s)
```
